```python
import jax
import jax.numpy as jnp
from jax import lax
import numpy as np

D_MODEL = 2048
BATCH = 8
SEQ = 2048
DEPTH = 1

D_MIX = D_MODEL
D_LRU = D_MIX // 2
D_RWKV = D_MIX - D_LRU
LRU_HEADS = 8
LRU_HEAD_DIM = D_LRU // LRU_HEADS
CONV_WIDTH = 4
LRU_C = 8.0
RWKV_HEAD_DIM = 64
RWKV_HEADS = D_RWKV // RWKV_HEAD_DIM
DECAY_LORA = 64
AAA_LORA = 64
GATE_LORA = 160
RWKV_COLS = 3 * D_RWKV + DECAY_LORA + AAA_LORA + GATE_LORA
IN_COLS = 2 * D_LRU + RWKV_COLS
PEER_HEADS = 8
PEER_NKEYS = 128
PEER_EXPERTS = PEER_NKEYS * PEER_NKEYS
PEER_DQ = 256
PEER_DHALF = PEER_DQ // 2
PEER_TOPK = 16
PEER_BLOCK = 128
NORM_EPS = 1e-6
GN_EPS = 64e-5

kernel_name = "hybrid_rglru_rwkv7_peer"


def rms_norm(x, g):
    xf = x.astype(jnp.float32)
    y = xf * lax.rsqrt(jnp.mean(xf * xf, axis=-1, keepdims=True) + NORM_EPS)
    return (y * g.astype(jnp.float32)).astype(x.dtype)


def rg_lru_group(xb, gate, conv_w, conv_b, wa, ba, wi, bi, lam):
    B, S, C = xb.shape
    xc = lax.conv_general_dilated(
        xb, conv_w.astype(xb.dtype)[:, None, :], window_strides=(1,),
        padding=((CONV_WIDTH - 1, 0),), dimension_numbers=("NWC", "WIO", "NWC"),
        feature_group_count=C) + conv_b
    xh = xc.reshape(B, S, LRU_HEADS, LRU_HEAD_DIM)
    r = jax.nn.sigmoid(jnp.einsum("bshi,hij->bshj", xh, wa).reshape(B, S, C) + ba)
    i = jax.nn.sigmoid(jnp.einsum("bshi,hij->bshj", xh, wi).reshape(B, S, C) + bi)
    log_a = -LRU_C * r.astype(jnp.float32) * jax.nn.softplus(-lam.astype(jnp.float32))
    a = jnp.exp(log_a)
    bx = jnp.sqrt(-jnp.expm1(2.0 * log_a)) * (i * xc).astype(jnp.float32)

    def combine(c1, c2):
        a1, b1 = c1
        a2, b2 = c2
        return a1 * a2, a2 * b1 + b2

    _, h = lax.associative_scan(combine, (a, bx), axis=1)
    return (h * jax.nn.gelu(gate.astype(jnp.float32))).astype(xb.dtype)


def rwkv7_group(p, mu, w0, w2, a0, a2, g2, k_k, k_a, r_k, lnx_w, lnx_b):
    B, S, _ = p.shape
    H, N = RWKV_HEADS, RWKV_HEAD_DIM
    prev = jnp.pad(p, ((0, 0), (1, 0), (0, 0)))[:, :-1]
    ps = (p + (prev - p) * mu).astype(jnp.float32)
    r, k, v, wd, ad, gd = jnp.split(
        ps, [D_RWKV, 2 * D_RWKV, 3 * D_RWKV, 3 * D_RWKV + DECAY_LORA,
             3 * D_RWKV + DECAY_LORA + AAA_LORA], axis=-1)
    w_log = -jax.nn.softplus(-(w0 + jnp.tanh(wd) @ w2)) - 0.5
    decay = jnp.exp(-jnp.exp(w_log))
    a = jax.nn.sigmoid(a0 + ad @ a2)
    g = jax.nn.sigmoid(gd) @ g2
    kk = (k * k_k).reshape(B, S, H, N)
    kk = kk / jnp.maximum(jnp.sqrt(jnp.sum(kk * kk, axis=-1, keepdims=True)), 1e-12)
    k = k * (1.0 + (a - 1.0) * k_a)
    heads = lambda t: t.reshape(B, S, H, N)
    r_h, k_h, v_h, w_h, a_h = heads(r), heads(k), heads(v), heads(decay), heads(a)
    b_h = kk * a_h

    def step(state, inp):
        r_t, w_t, k_t, v_t, na_t, b_t = inp
        sa = jnp.einsum("bhij,bhj->bhi", state, na_t)
        state = (state * w_t[:, :, None, :] + sa[..., None] * b_t[:, :, None, :]
                 + v_t[..., None] * k_t[:, :, None, :])
        return state, jnp.einsum("bhij,bhj->bhi", state, r_t)

    seq_first = lambda t: jnp.moveaxis(t, 1, 0)
    state0 = jnp.zeros((B, H, N, N), jnp.float32)
    _, y = lax.scan(step, state0, (seq_first(r_h), seq_first(w_h), seq_first(k_h),
                                   seq_first(v_h), seq_first(-kk), seq_first(b_h)))
    y = jnp.moveaxis(y, 0, 1)
    mean = jnp.mean(y, axis=-1, keepdims=True)
    var = jnp.mean(jnp.square(y - mean), axis=-1, keepdims=True)
    y = ((y - mean) * lax.rsqrt(var + GN_EPS)).reshape(B, S, D_RWKV) * lnx_w + lnx_b
    bonus = jnp.sum(r_h * k_h * r_k, axis=-1, keepdims=True) * v_h
    y = (y + bonus.reshape(B, S, D_RWKV)) * g
    return y.astype(p.dtype)


def peer_ffn(h, wq, keys, u_tab, v_tab):
    B, S, D = h.shape
    q = (h @ wq).astype(jnp.float32).reshape(B, S, PEER_HEADS, 2, PEER_DHALF)
    sub = jnp.einsum("bshpd,hpnd->bshpn", q, keys.astype(jnp.float32))
    sv, si = lax.top_k(sub, PEER_TOPK)
    cand = (sv[..., 0, :, None] + sv[..., 1, None, :]).reshape(
        B, S, PEER_HEADS, PEER_TOPK * PEER_TOPK)
    cid = (si[..., 0, :, None] * PEER_NKEYS + si[..., 1, None, :]).reshape(
        B, S, PEER_HEADS, PEER_TOPK * PEER_TOPK)
    best, pos = lax.top_k(cand, PEER_TOPK)
    experts = jnp.take_along_axis(cid, pos, axis=-1)
    gates = jax.nn.softmax(best, axis=-1)
    n_blk = (B * S) // PEER_BLOCK
    E = PEER_HEADS * PEER_TOPK
    xs = h.reshape(n_blk, PEER_BLOCK, D)
    es = experts.reshape(n_blk, PEER_BLOCK, E)
    gs = gates.reshape(n_blk, PEER_BLOCK, E).astype(h.dtype)

    def block(args):
        xb, eb, gb = args
        u = jnp.take(u_tab, eb, axis=0)
        act = jax.nn.gelu(jnp.einsum("cd,ced->ce", xb, u))
        vv = jnp.take(v_tab, eb, axis=0)
        return jnp.einsum("ce,ced->cd", gb * act, vv)

    return lax.map(block, (xs, es, gs)).reshape(B, S, D)


def hybrid_layer(x, norm1_g, w_in, conv_w, conv_b, lru_wa, lru_ba, lru_wi, lru_bi,
                 lru_lambda, rwkv_mu, rwkv_w0, rwkv_w2, rwkv_a0, rwkv_a2, rwkv_g2,
                 rwkv_k_k, rwkv_k_a, rwkv_r_k, rwkv_lnx_w, rwkv_lnx_b, w_out,
                 norm2_g, peer_wq, peer_keys, peer_u, peer_v):
    h = rms_norm(x, norm1_g)
    proj = h @ w_in
    y_lru = rg_lru_group(proj[..., :D_LRU], proj[..., D_LRU:2 * D_LRU], conv_w, conv_b,
                         lru_wa, lru_ba, lru_wi, lru_bi, lru_lambda)
    y_rwkv = rwkv7_group(proj[..., 2 * D_LRU:], rwkv_mu, rwkv_w0, rwkv_w2, rwkv_a0,
                         rwkv_a2, rwkv_g2, rwkv_k_k, rwkv_k_a, rwkv_r_k,
                         rwkv_lnx_w, rwkv_lnx_b)
    x = x + jnp.concatenate([y_lru, y_rwkv], axis=-1) @ w_out
    x = x + peer_ffn(rms_norm(x, norm2_g), peer_wq, peer_keys, peer_u, peer_v)
    return x


def setup_inputs(seed: int = 0) -> dict:
    key = jax.random.key(seed)
    ks = jax.random.split(key, 28)
    f32 = jnp.float32
    L = DEPTH

    def nrm(k, shape, std):
        return jax.random.normal(k, shape, f32) * std

    lam_u = jax.random.uniform(ks[10], (L, D_LRU), f32, 0.9, 0.999)
    lam_s = lam_u ** (1.0 / LRU_C)
    lru_lambda = jnp.log(lam_s) - jnp.log1p(-lam_s)
    ramp = jnp.linspace(0.0, 1.0, D_RWKV, dtype=f32)
    rwkv_w0 = -6.5 + 5.0 * ramp ** 0.85 + nrm(ks[12], (L, D_RWKV), 0.1)
    return {
        "x": nrm(ks[0], (BATCH, SEQ, D_MODEL), 1.0),
        "norm1_g": 1.0 + nrm(ks[1], (L, D_MODEL), 0.02),
        "w_in": nrm(ks[2], (L, D_MODEL, IN_COLS), D_MODEL ** -0.5),
        "conv_w": nrm(ks[3], (L, CONV_WIDTH, D_LRU), CONV_WIDTH ** -0.5),
        "conv_b": nrm(ks[4], (L, D_LRU), 0.02),
        "lru_wa": nrm(ks[5], (L, LRU_HEADS, LRU_HEAD_DIM, LRU_HEAD_DIM), LRU_HEAD_DIM ** -0.5),
        "lru_ba": nrm(ks[6], (L, D_LRU), 0.02),
        "lru_wi": nrm(ks[7], (L, LRU_HEADS, LRU_HEAD_DIM, LRU_HEAD_DIM), LRU_HEAD_DIM ** -0.5),
        "lru_bi": nrm(ks[8], (L, D_LRU), 0.02),
        "lru_lambda": lru_lambda,
        "rwkv_mu": jax.random.uniform(ks[11], (L, RWKV_COLS), f32),
        "rwkv_w0": rwkv_w0,
        "rwkv_w2": nrm(ks[13], (L, DECAY_LORA, D_RWKV), 0.1 * DECAY_LORA ** -0.5),
        "rwkv_a0": nrm(ks[14], (L, D_RWKV), 0.1),
        "rwkv_a2": nrm(ks[15], (L, AAA_LORA, D_RWKV), AAA_LORA ** -0.5),
        "rwkv_g2": nrm(ks[16], (L, GATE_LORA, D_RWKV), GATE_LORA ** -0.5),
        "rwkv_k_k": 0.85 + nrm(ks[17], (L, D_RWKV), 0.02),
        "rwkv_k_a": 1.0 + nrm(ks[18], (L, D_RWKV), 0.02),
        "rwkv_r_k": nrm(ks[19], (L, RWKV_HEADS, RWKV_HEAD_DIM), 0.1),
        "rwkv_lnx_w": 1.0 + nrm(ks[20], (L, D_RWKV), 0.02),
        "rwkv_lnx_b": nrm(ks[21], (L, D_RWKV), 0.02),
        "w_out": nrm(ks[22], (L, D_MIX, D_MODEL), D_MIX ** -0.5),
        "norm2_g": 1.0 + nrm(ks[23], (L, D_MODEL), 0.02),
        "peer_wq": nrm(ks[24], (L, D_MODEL, PEER_HEADS * PEER_DQ), D_MODEL ** -0.5),
        "peer_keys": nrm(ks[25], (L, PEER_HEADS, 2, PEER_NKEYS, PEER_DHALF), PEER_DHALF ** -0.5),
        "peer_u": nrm(ks[26], (L, PEER_EXPERTS, D_MODEL), D_MODEL ** -0.5),
        "peer_v": nrm(ks[27], (L, PEER_EXPERTS, D_MODEL), 0.5),
        "normf_g": 1.0 + nrm(ks[9], (D_MODEL,), 0.02),
    }


def reference(x, norm1_g, w_in, conv_w, conv_b, lru_wa, lru_ba, lru_wi, lru_bi,
              lru_lambda, rwkv_mu, rwkv_w0, rwkv_w2, rwkv_a0, rwkv_a2, rwkv_g2,
              rwkv_k_k, rwkv_k_a, rwkv_r_k, rwkv_lnx_w, rwkv_lnx_b, w_out, norm2_g,
              peer_wq, peer_keys, peer_u, peer_v, normf_g):
    for l in range(DEPTH):
        x = hybrid_layer(x, norm1_g[l], w_in[l], conv_w[l], conv_b[l], lru_wa[l],
                         lru_ba[l], lru_wi[l], lru_bi[l], lru_lambda[l], rwkv_mu[l],
                         rwkv_w0[l], rwkv_w2[l], rwkv_a0[l], rwkv_a2[l], rwkv_g2[l],
                         rwkv_k_k[l], rwkv_k_a[l], rwkv_r_k[l], rwkv_lnx_w[l],
                         rwkv_lnx_b[l], w_out[l], norm2_g[l], peer_wq[l],
                         peer_keys[l], peer_u[l], peer_v[l])
    return rms_norm(x, normf_g)
```

```python
import functools

import jax
import jax.numpy as jnp
import numpy as np
from jax import lax
from jax.experimental import pallas as pl
from jax.experimental.pallas import tpu as pltpu

NORM_EPS = 1e-6
GN_EPS = 64e-5
LRU_C = 8.0
LRU_HEADS = 8
CONV_WIDTH = 4
RWKV_HEAD_DIM = 64
DECAY_LORA = 64
AAA_LORA = 64
GATE_LORA = 160
PEER_HEADS = 8
PEER_NKEYS = 128
PEER_TOPK = 16

LANES = 128
SUBLANES = 8
LORA_PAD = 512
W_PITCH = 136
VMEM_LIMIT = 48 * 1024 * 1024

F32 = jnp.float32
BF16 = jnp.bfloat16


def _params(n_axes):
    return pltpu.CompilerParams(dimension_semantics=("arbitrary",) * n_axes,
                                vmem_limit_bytes=VMEM_LIMIT)


def _softplus(z):
    return jnp.maximum(z, 0.0) + jnp.log1p(jnp.exp(-jnp.abs(z)))


def _split_bf16(x):
    hi = x.astype(BF16)
    lo = (x - hi.astype(F32)).astype(BF16)
    return hi, lo


def _segsum(x, e_blk):
    outs = []
    for c in range(x.shape[1] // LANES):
        hi, lo = _split_bf16(x[:, c * LANES:(c + 1) * LANES])
        outs.append(jnp.dot(hi, e_blk, preferred_element_type=F32)
                    + jnp.dot(lo, e_blk, preferred_element_type=F32))
    return jnp.concatenate(outs, axis=1)


def _inproj_kernel(x_ref, g_ref, w_ref, o_ref, h_ref):
    @pl.when(pl.program_id(1) == 0)
    def _():
        x = x_ref[...]
        ms = jnp.mean(x * x, axis=-1, keepdims=True)
        h_ref[...] = (x * lax.rsqrt(ms + NORM_EPS) * g_ref[...]).astype(BF16)

    o_ref[...] = jnp.dot(h_ref[...], w_ref[...], preferred_element_type=F32)


def _inproj(x2, g, w_bf, tm, tn):
    t, d = x2.shape
    n = w_bf.shape[1]
    return pl.pallas_call(
        _inproj_kernel,
        grid=(t // tm, n // tn),
        in_specs=[pl.BlockSpec((tm, d), lambda i, j: (i, 0)),
                  pl.BlockSpec((1, d), lambda i, j: (0, 0)),
                  pl.BlockSpec((d, tn), lambda i, j: (0, j))],
        out_specs=pl.BlockSpec((tm, tn), lambda i, j: (i, j)),
        out_shape=jax.ShapeDtypeStruct((t, n), F32),
        scratch_shapes=[pltpu.VMEM((tm, d), BF16)],
        compiler_params=_params(2),
        name="inproj",
    )(x2, g, w_bf)


def _lru_kernel(xb_ref, gate_ref, cw_ref, cb_ref, wa_ref, ba_ref, wi_ref, bi_ref, lam_ref,
                o_ref, ext_ref, a_ref, b_ref, hcar_ref):
    tt, c = xb_ref.shape
    hd = c // LRU_HEADS

    @pl.when(pl.program_id(1) == 0)
    def _():
        ext_ref[0:SUBLANES, :] = jnp.zeros((SUBLANES, c), F32)
        hcar_ref[...] = jnp.zeros((1, c), F32)

    x = xb_ref[...]
    ext_ref[SUBLANES:SUBLANES + tt, :] = x
    xc = cb_ref[...] + cw_ref[CONV_WIDTH - 1:CONV_WIDTH, :] * x
    for k in range(1, CONV_WIDTH):
        xc = xc + cw_ref[CONV_WIDTH - 1 - k:CONV_WIDTH - k, :] * ext_ref[SUBLANES - k:SUBLANES - k + tt, :]
    ext_ref[0:SUBLANES, :] = x[tt - SUBLANES:tt, :]

    xcb = xc.astype(BF16)
    ra, ia = [], []
    for h in range(LRU_HEADS):
        xh = xcb[:, h * hd:(h + 1) * hd]
        ra.append(jnp.dot(xh, wa_ref[h], preferred_element_type=F32))
        ia.append(jnp.dot(xh, wi_ref[h], preferred_element_type=F32))
    r = jax.nn.sigmoid(jnp.concatenate(ra, axis=1) + ba_ref[...])
    ig = jax.nn.sigmoid(jnp.concatenate(ia, axis=1) + bi_ref[...])
    log_a = (-LRU_C) * r * _softplus(-lam_ref[...])
    a = jnp.exp(log_a)
    bx = jnp.sqrt(-jnp.tanh(log_a) * (a * a + 1.0)) * (ig * xc)

    row = lax.broadcasted_iota(jnp.int32, (tt, c), 0) % SUBLANES
    for d in (1, 2, 4):
        m = row >= d
        a_sh = jnp.where(m, pltpu.roll(a, d, 0), 1.0)
        b_sh = jnp.where(m, pltpu.roll(bx, d, 0), 0.0)
        bx = a * b_sh + bx
        a = a * a_sh
    a_ref[...] = a
    b_ref[...] = bx

    def blk(i, hc):
        off = pl.multiple_of(i * SUBLANES, SUBLANES)
        hb = a_ref[pl.ds(off, SUBLANES), :] * hc + b_ref[pl.ds(off, SUBLANES), :]
        b_ref[pl.ds(off, SUBLANES), :] = hb
        return hb[SUBLANES - 1:SUBLANES, :]

    hcar_ref[...] = lax.fori_loop(0, tt // SUBLANES, blk, hcar_ref[...])
    o_ref[...] = (b_ref[...] * jax.nn.gelu(gate_ref[...])).astype(BF16)


def _lru(proj, b, s, tt, cw, cb, wa, ba, wi, bi, lam):
    c = cw.shape[1]
    nt = s // tt
    row = lambda bb, ss: (bb * nt + ss, 0)
    const2 = lambda bb, ss: (0, 0)
    const3 = lambda bb, ss: (0, 0, 0)
    hd = c // LRU_HEADS
    return pl.pallas_call(
        _lru_kernel,
        grid=(b, nt),
        in_specs=[pl.BlockSpec((tt, c), row),
                  pl.BlockSpec((tt, c), lambda bb, ss: (bb * nt + ss, 1)),
                  pl.BlockSpec((CONV_WIDTH, c), const2),
                  pl.BlockSpec((1, c), const2),
                  pl.BlockSpec((LRU_HEADS, hd, hd), const3),
                  pl.BlockSpec((1, c), const2),
                  pl.BlockSpec((LRU_HEADS, hd, hd), const3),
                  pl.BlockSpec((1, c), const2),
                  pl.BlockSpec((1, c), const2)],
        out_specs=pl.BlockSpec((tt, c), row),
        out_shape=jax.ShapeDtypeStruct((b * s, c), BF16),
        scratch_shapes=[pltpu.VMEM((tt + SUBLANES, c), F32),
                        pltpu.VMEM((tt, c), F32),
                        pltpu.VMEM((tt, c), F32),
                        pltpu.VMEM((1, c), F32)],
        compiler_params=_params(2),
        name="rg_lru",
    )(proj, proj, cw, cb, wa, ba, wi, bi, lam)


def _token_shift(ext_ref, x, mu):
    tt = x.shape[0]
    ext_ref[SUBLANES:SUBLANES + tt, :] = x
    prev = ext_ref[SUBLANES - 1:SUBLANES - 1 + tt, :]
    ext_ref[0:SUBLANES, :] = x[tt - SUBLANES:tt, :]
    return x + (prev - x) * mu


def _rwkv_prep_kernel(r_ref, k_ref, v_ref, l_ref, mur_ref, muk_ref, muv_ref, mul_ref,
                      w0_ref, w2_ref, a0_ref, a2_ref, g2_ref, kk_ref, ka_ref, rk_ref, e_ref,
                      ro_ref, wo_ref, ko_ref, vo_ref, nao_ref, bo_ref, go_ref, bonus_ref,
                      er_ref, ek_ref, ev_ref, el_ref):
    @pl.when(pl.program_id(1) == 0)
    def _():
        for ext in (er_ref, ek_ref, ev_ref, el_ref):
            ext[0:SUBLANES, :] = jnp.zeros((SUBLANES, ext.shape[1]), F32)

    r = _token_shift(er_ref, r_ref[...], mur_ref[...])
    k = _token_shift(ek_ref, k_ref[...], muk_ref[...])
    v = _token_shift(ev_ref, v_ref[...], muv_ref[...])
    lo = _token_shift(el_ref, l_ref[...], mul_ref[...])

    lo_a = lo[:, 0:LANES]
    lo_g = lo[:, LANES:3 * LANES]
    wl = w0_ref[...] + jnp.dot(jnp.tanh(lo_a).astype(BF16), w2_ref[...], preferred_element_type=F32)
    w_log = -_softplus(-wl) - 0.5
    decay = jnp.exp(-jnp.exp(w_log))
    a = jax.nn.sigmoid(a0_ref[...] + jnp.dot(lo_a.astype(BF16), a2_ref[...], preferred_element_type=F32))
    g = jnp.dot(jax.nn.sigmoid(lo_g).astype(BF16), g2_ref[...], preferred_element_type=F32)

    e_blk = e_ref[...]
    kk = k * kk_ref[...]
    kk = kk / jnp.maximum(jnp.sqrt(_segsum(kk * kk, e_blk)), 1e-12)
    kmod = k * (1.0 + (a - 1.0) * ka_ref[...])

    ro_ref[...] = r
    wo_ref[...] = decay
    ko_ref[...] = kmod
    vo_ref[...] = v
    nao_ref[...] = -kk
    bo_ref[...] = kk * a
    go_ref[...] = g
    bonus_ref[...] = _segsum(r * kmod * rk_ref[...], e_blk) * v


def _rwkv_prep(proj, b, s, tt, c, col0, mur, muk, muv, mul, w0, w2p, a0, a2p, g2p, kk, ka, rk, e_blk):
    nt = s // tt
    cb0 = col0 // c
    lb = (col0 + 3 * c) // LORA_PAD
    const2 = lambda bb, ss: (0, 0)
    vec = pl.BlockSpec((1, c), const2)
    row = lambda bb, ss: (bb * nt + ss, 0)
    out = jax.ShapeDtypeStruct((b * s, c), F32)
    return pl.pallas_call(
        _rwkv_prep_kernel,
        grid=(b, nt),
        in_specs=[pl.BlockSpec((tt, c), lambda bb, ss: (bb * nt + ss, cb0)),
                  pl.BlockSpec((tt, c), lambda bb, ss: (bb * nt + ss, cb0 + 1)),
                  pl.BlockSpec((tt, c), lambda bb, ss: (bb * nt + ss, cb0 + 2)),
                  pl.BlockSpec((tt, LORA_PAD), lambda bb, ss: (bb * nt + ss, lb)),
                  vec, vec, vec, pl.BlockSpec((1, LORA_PAD), const2),
                  vec, pl.BlockSpec((LANES, c), const2),
                  vec, pl.BlockSpec((LANES, c), const2),
                  pl.BlockSpec((2 * LANES, c), const2),
                  vec, vec, vec, pl.BlockSpec((LANES, LANES), const2)],
        out_specs=[pl.BlockSpec((tt, c), row)] * 8,
        out_shape=[out] * 8,
        scratch_shapes=[pltpu.VMEM((tt + SUBLANES, c), F32),
                        pltpu.VMEM((tt + SUBLANES, c), F32),
                        pltpu.VMEM((tt + SUBLANES, c), F32),
                        pltpu.VMEM((tt + SUBLANES, LORA_PAD), F32)],
        compiler_params=_params(2),
        name="rwkv_prep",
    )(proj, proj, proj, proj, mur, muk, muv, mul, w0, w2p, a0, a2p, g2p, kk, ka, rk, e_blk)


def _rwkv_scan_kernel(r_ref, w_ref, k_ref, v_ref, na_ref, b_ref, y_ref, s_ref, sa_ref):
    tt, n, _ = r_ref.shape

    @pl.when(pl.program_id(0) == 0)
    def _():
        s_ref[...] = jnp.zeros(s_ref.shape, F32)

    def step(t, carry):
        w = w_ref[t]
        k = k_ref[t]
        b = b_ref[t]
        r = r_ref[t]
        na = na_ref[t]
        for i in range(n):
            sa_ref[i:i + 1, :] = jnp.sum(s_ref[i] * na, axis=0, keepdims=True)
        for i in range(n):
            s_new = s_ref[i] * w + sa_ref[i:i + 1, :] * b + v_ref[t, i:i + 1, :] * k
            s_ref[i] = s_new
            y_ref[t, i:i + 1, :] = jnp.sum(s_new * r, axis=0, keepdims=True)
        y = y_ref[t]
        mean = jnp.mean(y, axis=0, keepdims=True)
        yc = y - mean
        var = jnp.mean(yc * yc, axis=0, keepdims=True)
        y_ref[t] = yc * lax.rsqrt(var + GN_EPS)
        return carry

    lax.fori_loop(0, tt, step, 0)


def _rwkv_scan(r, w, k, v, na, b, tt):
    s, n, bh = r.shape
    blk = pl.BlockSpec((tt, n, bh), lambda i: (i, 0, 0))
    return pl.pallas_call(
        _rwkv_scan_kernel,
        grid=(s // tt,),
        in_specs=[blk] * 6,
        out_specs=blk,
        out_shape=jax.ShapeDtypeStruct((s, n, bh), F32),
        scratch_shapes=[pltpu.VMEM((n, n, bh), F32), pltpu.VMEM((n, bh), F32)],
        compiler_params=_params(1),
        name="rwkv_scan",
    )(r, w, k, v, na, b)


def _outproj_kernel(x_ref, ylru_ref, yn_ref, bonus_ref, g_ref, lw_ref, lb_ref, wo1_ref, wo2_ref,
                    n2_ref, x1_ref, h2_ref):
    y_rwkv = ((yn_ref[...] * lw_ref[...] + lb_ref[...] + bonus_ref[...]) * g_ref[...]).astype(BF16)
    x1 = (x_ref[...]
          + jnp.dot(ylru_ref[...], wo1_ref[...], preferred_element_type=F32)
          + jnp.dot(y_rwkv, wo2_ref[...], preferred_element_type=F32))
    x1_ref[...] = x1
    ms = jnp.mean(x1 * x1, axis=-1, keepdims=True)
    h2_ref[...] = (x1 * lax.rsqrt(ms + NORM_EPS) * n2_ref[...]).astype(BF16)


def _outproj(x2, ylru, yn, bonus, g, lw, lb, wo1, wo2, n2, tm):
    t, d = x2.shape
    c = ylru.shape[1]
    row_d = pl.BlockSpec((tm, d), lambda i: (i, 0))
    row_c = pl.BlockSpec((tm, c), lambda i: (i, 0))
    const = lambda i: (0, 0)
    return pl.pallas_call(
        _outproj_kernel,
        grid=(t // tm,),
        in_specs=[row_d, row_c, row_c, row_c, row_c,
                  pl.BlockSpec((1, c), const), pl.BlockSpec((1, c), const),
                  pl.BlockSpec((c, d), const), pl.BlockSpec((c, d), const),
                  pl.BlockSpec((1, d), const)],
        out_specs=[row_d, row_d],
        out_shape=[jax.ShapeDtypeStruct((t, d), F32), jax.ShapeDtypeStruct((t, d), BF16)],
        compiler_params=_params(1),
        name="outproj",
    )(x2, ylru, yn, bonus, g, lw, lb, wo1, wo2, n2)


def _matmul_kernel(a_ref, w_ref, o_ref):
    o_ref[...] = jnp.dot(a_ref[...], w_ref[...], preferred_element_type=F32)


def _qproj(h2, wq, tm):
    t, d = h2.shape
    n = wq.shape[1]
    return pl.pallas_call(
        _matmul_kernel,
        grid=(t // tm,),
        in_specs=[pl.BlockSpec((tm, d), lambda i: (i, 0)),
                  pl.BlockSpec((d, n), lambda i: (0, 0))],
        out_specs=pl.BlockSpec((tm, n), lambda i: (i, 0)),
        out_shape=jax.ShapeDtypeStruct((t, n), F32),
        compiler_params=_params(1),
        name="qproj",
    )(h2, wq)


def _dot_nt3(a, b):
    dn = (((1,), (1,)), ((), ()))
    ah, al = _split_bf16(a)
    bh, bl = _split_bf16(b)
    return (lax.dot_general(ah, bh, dn, preferred_element_type=F32)
            + lax.dot_general(ah, bl, dn, preferred_element_type=F32)
            + lax.dot_general(al, bh, dn, preferred_element_type=F32))


def _tree(op, xs):
    xs = list(xs)
    while len(xs) > 1:
        nxt = [op(xs[i], xs[i + 1]) for i in range(0, len(xs) - 1, 2)]
        if len(xs) % 2:
            nxt.append(xs[-1])
        xs = nxt
    return xs[0]


def _route_kernel(q_ref, keys_ref, g_ref, ii_ref, jj_ref):
    tm = q_ref.shape[0]
    nk = PEER_NKEYS
    neg = -jnp.inf
    iota_n = lax.broadcasted_iota(jnp.int32, (nk, tm), 0).astype(F32)

    vals = [[None, None] for _ in range(PEER_HEADS)]
    idxs = [[None, None] for _ in range(PEER_HEADS)]
    for h in range(PEER_HEADS):
        for p in range(2):
            hp = 2 * h + p
            cur = _dot_nt3(keys_ref[hp], q_ref[:, hp * nk:(hp + 1) * nk])
            vs, ix = [], []
            for _ in range(PEER_TOPK):
                m = jnp.max(cur, axis=0, keepdims=True)
                idx = jnp.min(jnp.where(cur == m, iota_n, float(nk)), axis=0, keepdims=True)
                vs.append(m)
                ix.append(idx)
                cur = jnp.where(iota_n == idx, neg, cur)
            vals[h][p] = vs
            idxs[h][p] = ix

    def by_rank(table, p, rnk):
        return jnp.concatenate([table[h][p][rnk] for h in range(PEER_HEADS)], axis=0)

    va = [by_rank(vals, 0, r) for r in range(PEER_TOPK)]
    vb = [by_rank(vals, 1, r) for r in range(PEER_TOPK)]
    ia = [by_rank(idxs, 0, r) for r in range(PEER_TOPK)]
    ib = [by_rank(idxs, 1, r) for r in range(PEER_TOPK)]

    cand = []
    for p in range(PEER_TOPK):
        for qq in range(PEER_TOPK // (p + 1)):
            cand.append([va[p] + vb[qq], ia[p], ib[qq]])
    ncand = len(cand)

    best, sel_i, sel_j = [], [], []
    for _ in range(PEER_TOPK):
        m = _tree(jnp.maximum, [c[0] for c in cand])
        code = _tree(jnp.minimum, [jnp.where(c[0] == m, float(ci), float(ncand))
                                   for ci, c in enumerate(cand)])
        ci_sel = jnp.zeros_like(m)
        cj_sel = jnp.zeros_like(m)
        for ci, c in enumerate(cand):
            win = code == float(ci)
            ci_sel = jnp.where(win, c[1], ci_sel)
            cj_sel = jnp.where(win, c[2], cj_sel)
            c[0] = jnp.where(win, neg, c[0])
        best.append(m)
        sel_i.append(ci_sel)
        sel_j.append(cj_sel)

    ex = [jnp.exp(bv - best[0]) for bv in best]
    z = _tree(jnp.add, ex)
    gates = jnp.concatenate([e / z for e in ex], axis=0)
    g_ref[...] = gates.T
    ii_ref[...] = jnp.concatenate(sel_i, axis=0).T
    jj_ref[...] = jnp.concatenate(sel_j, axis=0).T


def _route(q, keys, tm):
    t, dq = q.shape
    nslot = PEER_TOPK * PEER_HEADS
    out = jax.ShapeDtypeStruct((t, nslot), F32)
    oblk = pl.BlockSpec((tm, nslot), lambda i: (i, 0))
    return pl.pallas_call(
        _route_kernel,
        grid=(t // tm,),
        in_specs=[pl.BlockSpec((tm, dq), lambda i: (i, 0)),
                  pl.BlockSpec(keys.shape, lambda i: (0, 0, 0))],
        out_specs=[oblk, oblk, oblk],
        out_shape=[out, out, out],
        compiler_params=_params(1),
        name="peer_route",
    )(q, keys)


def _wbuild_kernel(g_ref, ii_ref, jj_ref, w_ref, buf_ref):
    tw = g_ref.shape[0]
    nk = PEER_NKEYS
    iota_s = lax.broadcasted_iota(jnp.int32, (nk, g_ref.shape[1]), 0).astype(F32)
    dn = (((1,), (1,)), ((), ()))

    def tok(t, carry):
        g = g_ref[pl.ds(t, 1), :]
        a = jnp.where(iota_s == ii_ref[pl.ds(t, 1), :], g, 0.0).astype(BF16)
        bt = jnp.where(iota_s == jj_ref[pl.ds(t, 1), :], 1.0, 0.0).astype(BF16)
        off = pl.multiple_of(t * W_PITCH, SUBLANES)
        buf_ref[pl.ds(off, nk), :] = lax.dot_general(a, bt, dn, preferred_element_type=F32)
        return carry

    lax.fori_loop(0, tw, tok, 0)
    for i in range(nk):
        w_ref[:, i * nk:(i + 1) * nk] = buf_ref[pl.ds(i, tw, stride=W_PITCH), :].astype(BF16)


def _wbuild(gates, ii, jj, tw):
    t, nslot = gates.shape
    ne = PEER_NKEYS * PEER_NKEYS
    blk = pl.BlockSpec((tw, nslot), lambda i: (i, 0))
    return pl.pallas_call(
        _wbuild_kernel,
        grid=(t // tw,),
        in_specs=[blk, blk, blk],
        out_specs=pl.BlockSpec((tw, ne), lambda i: (i, 0)),
        out_shape=jax.ShapeDtypeStruct((t, ne), BF16),
        scratch_shapes=[pltpu.VMEM((tw * W_PITCH, PEER_NKEYS), F32)],
        compiler_params=_params(1),
        name="peer_weights",
    )(gates, ii, jj)


def _peer_kernel(h_ref, u_ref, v_ref, w_ref, o_ref):
    @pl.when(pl.program_id(1) == 0)
    def _():
        o_ref[...] = jnp.zeros(o_ref.shape, F32)

    act = lax.dot_general(h_ref[...], u_ref[...], (((1,), (1,)), ((), ())),
                          preferred_element_type=F32)
    p = (jax.nn.gelu(act) * w_ref[...].astype(F32)).astype(BF16)
    o_ref[...] += jnp.dot(p, v_ref[...], preferred_element_type=F32)


def _peer(h2, u_bf, v_bf, w, tm, te):
    t, d = h2.shape
    ne = u_bf.shape[0]
    return pl.pallas_call(
        _peer_kernel,
        grid=(t // tm, ne // te),
        in_specs=[pl.BlockSpec((tm, d), lambda i, e: (i, 0)),
                  pl.BlockSpec((te, d), lambda i, e: (e, 0)),
                  pl.BlockSpec((te, d), lambda i, e: (e, 0)),
                  pl.BlockSpec((tm, te), lambda i, e: (i, e))],
        out_specs=pl.BlockSpec((tm, d), lambda i, e: (i, 0)),
        out_shape=jax.ShapeDtypeStruct((t, d), F32),
        compiler_params=_params(2),
        name="peer_experts",
    )(h2, u_bf, v_bf, w)


def _final_kernel(x1_ref, p_ref, g_ref, o_ref):
    x = x1_ref[...] + p_ref[...]
    ms = jnp.mean(x * x, axis=-1, keepdims=True)
    o_ref[...] = x * lax.rsqrt(ms + NORM_EPS) * g_ref[...]


def _final(x1, peer, g, tm):
    t, d = x1.shape
    row = pl.BlockSpec((tm, d), lambda i: (i, 0))
    return pl.pallas_call(
        _final_kernel,
        grid=(t // tm,),
        in_specs=[row, row, pl.BlockSpec((1, d), lambda i: (0, 0))],
        out_specs=row,
        out_shape=jax.ShapeDtypeStruct((t, d), F32),
        compiler_params=_params(1),
        name="final_norm",
    )(x1, peer, g)


def _tiles(t, s):
    return dict(
        inproj_tm=min(1024, t), inproj_tn=512,
        seq_tt=min(256, s),
        scan_tt=min(16, s),
        outproj_tm=min(256, t),
        qproj_tm=min(512, t),
        route_tm=min(128, t),
        wbuild_tw=min(128, t),
        peer_tm=min(1024, t), peer_te=512,
        final_tm=min(512, t),
    )


def _layer(x2, b, s, norm1_g, w_in, conv_w, conv_b, lru_wa, lru_ba, lru_wi, lru_bi, lru_lambda,
           rwkv_mu, rwkv_w0, rwkv_w2, rwkv_a0, rwkv_a2, rwkv_g2, rwkv_k_k, rwkv_k_a, rwkv_r_k,
           rwkv_lnx_w, rwkv_lnx_b, w_out, norm2_g, peer_wq, peer_keys, peer_u, peer_v):
    t, d = x2.shape
    c_lru = conv_w.shape[1]
    c_rwkv = rwkv_w0.shape[0]
    n_heads = c_rwkv // RWKV_HEAD_DIM
    tl = _tiles(t, s)
    row = lambda v: v.reshape(1, -1)

    n_main = 2 * c_lru + 3 * c_rwkv
    n_lora = DECAY_LORA + AAA_LORA + GATE_LORA
    w_in_p = jnp.pad(w_in, ((0, 0), (0, LORA_PAD - n_lora))).astype(BF16)
    proj = _inproj(x2, row(norm1_g), w_in_p, tl["inproj_tm"], tl["inproj_tn"])

    y_lru = _lru(proj, b, s, tl["seq_tt"], conv_w, row(conv_b), lru_wa.astype(BF16), row(lru_ba),
                 lru_wi.astype(BF16), row(lru_bi), row(lru_lambda))

    mu_main, mu_lora = rwkv_mu[:3 * c_rwkv], rwkv_mu[3 * c_rwkv:]
    mu_l = jnp.pad(mu_lora, (0, LORA_PAD - n_lora))
    w2p = jnp.pad(rwkv_w2, ((0, LANES - DECAY_LORA), (0, 0))).astype(BF16)
    a2p = jnp.pad(rwkv_a2, ((DECAY_LORA, LANES - DECAY_LORA - AAA_LORA), (0, 0))).astype(BF16)
    g2p = jnp.pad(rwkv_g2, ((0, 2 * LANES - GATE_LORA), (0, 0))).astype(BF16)
    seg = np.arange(LANES) // RWKV_HEAD_DIM
    e_blk = jnp.asarray(seg[:, None] == seg[None, :], dtype=BF16)
    r_s, w_s, k_s, v_s, na_s, b_s, g_s, bonus = _rwkv_prep(
        proj, b, s, tl["seq_tt"], c_rwkv, 2 * c_lru,
        row(mu_main[:c_rwkv]), row(mu_main[c_rwkv:2 * c_rwkv]), row(mu_main[2 * c_rwkv:]), row(mu_l),
        row(rwkv_w0), w2p, row(rwkv_a0), a2p, g2p, row(rwkv_k_k), row(rwkv_k_a),
        row(rwkv_r_k), e_blk)

    def to_scan(a):
        return a.reshape(b, s, n_heads, RWKV_HEAD_DIM).transpose(1, 3, 0, 2).reshape(
            s, RWKV_HEAD_DIM, b * n_heads)

    yn = _rwkv_scan(to_scan(r_s), to_scan(w_s), to_scan(k_s), to_scan(v_s), to_scan(na_s),
                    to_scan(b_s), tl["scan_tt"])
    yn = yn.reshape(s, RWKV_HEAD_DIM, b, n_heads).transpose(2, 0, 3, 1).reshape(t, c_rwkv)

    w_out_bf = w_out.astype(BF16)
    x1, h2 = _outproj(x2, y_lru, yn, bonus, g_s, row(rwkv_lnx_w), row(rwkv_lnx_b),
                      w_out_bf[:c_lru], w_out_bf[c_lru:], row(norm2_g), tl["outproj_tm"])

    q = _qproj(h2, peer_wq.astype(BF16), tl["qproj_tm"])
    keys = peer_keys.reshape(2 * PEER_HEADS, PEER_NKEYS, -1)
    gates, ii, jj = _route(q, keys, tl["route_tm"])
    w = _wbuild(gates, ii, jj, tl["wbuild_tw"])
    peer = _peer(h2, peer_u.astype(BF16), peer_v.astype(BF16), w, tl["peer_tm"], tl["peer_te"])
    return x1, peer


def kernel(x, norm1_g, w_in, conv_w, conv_b, lru_wa, lru_ba, lru_wi, lru_bi, lru_lambda, rwkv_mu, rwkv_w0, rwkv_w2, rwkv_a0, rwkv_a2, rwkv_g2, rwkv_k_k, rwkv_k_a, rwkv_r_k, rwkv_lnx_w, rwkv_lnx_b, w_out, norm2_g, peer_wq, peer_keys, peer_u, peer_v, normf_g):
    b, s, d = x.shape
    depth = norm1_g.shape[0]
    x2 = x.reshape(b * s, d)
    peer = jnp.zeros_like(x2)
    for l in range(depth):
        if l > 0:
            x2 = x2 + peer
        x2, peer = _layer(x2, b, s, norm1_g[l], w_in[l], conv_w[l], conv_b[l], lru_wa[l], lru_ba[l],
                          lru_wi[l], lru_bi[l], lru_lambda[l], rwkv_mu[l], rwkv_w0[l], rwkv_w2[l],
                          rwkv_a0[l], rwkv_a2[l], rwkv_g2[l], rwkv_k_k[l], rwkv_k_a[l],
                          rwkv_r_k[l].reshape(-1), rwkv_lnx_w[l], rwkv_lnx_b[l], w_out[l], norm2_g[l],
                          peer_wq[l], peer_keys[l], peer_u[l], peer_v[l])
    out = _final(x2, peer, normf_g.reshape(1, -1), _tiles(b * s, s)["final_tm"])
    return out.reshape(b, s, d)
```

```python
import functools

import jax
import jax.numpy as jnp
import numpy as np
from jax import lax
from jax.experimental import pallas as pl
from jax.experimental.pallas import tpu as pltpu

NORM_EPS = 1e-6
GN_EPS = 64e-5
LRU_C = 8.0
LRU_HEADS = 8
CONV_WIDTH = 4
RWKV_HEAD_DIM = 64
DECAY_LORA = 64
AAA_LORA = 64
GATE_LORA = 160
PEER_HEADS = 8
PEER_NKEYS = 128
PEER_TOPK = 16

LANES = 128
SUBLANES = 8
LORA_PAD = 512
W_PITCH = 136
N_PITCH = 72
N_SCAN_IN = 6
TOKEN_UNROLL = 8
VMEM_LIMIT = 48 * 1024 * 1024

F32 = jnp.float32
BF16 = jnp.bfloat16


def _params(n_axes):
    return pltpu.CompilerParams(dimension_semantics=("arbitrary",) * n_axes,
                                vmem_limit_bytes=VMEM_LIMIT)


def _softplus(z):
    return jnp.maximum(z, 0.0) + jnp.log1p(jnp.exp(-jnp.abs(z)))


def _split_bf16(x):
    hi = x.astype(BF16)
    lo = (x - hi.astype(F32)).astype(BF16)
    return hi, lo


def _segsum(x, e_blk):
    outs = []
    for c in range(x.shape[1] // LANES):
        hi, lo = _split_bf16(x[:, c * LANES:(c + 1) * LANES])
        outs.append(jnp.dot(hi, e_blk, preferred_element_type=F32)
                    + jnp.dot(lo, e_blk, preferred_element_type=F32))
    return jnp.concatenate(outs, axis=1)


def _inproj_kernel(x_ref, g_ref, w_ref, o_ref, h_ref):
    @pl.when(pl.program_id(1) == 0)
    def _():
        x = x_ref[...]
        ms = jnp.mean(x * x, axis=-1, keepdims=True)
        h_ref[...] = (x * lax.rsqrt(ms + NORM_EPS) * g_ref[...]).astype(BF16)

    o_ref[...] = jnp.dot(h_ref[...], w_ref[...], preferred_element_type=F32)


def _inproj(x2, g, w_bf, tm, tn):
    t, d = x2.shape
    n = w_bf.shape[1]
    return pl.pallas_call(
        _inproj_kernel,
        grid=(t // tm, n // tn),
        in_specs=[pl.BlockSpec((tm, d), lambda i, j: (i, 0)),
                  pl.BlockSpec((1, d), lambda i, j: (0, 0)),
                  pl.BlockSpec((d, tn), lambda i, j: (0, j))],
        out_specs=pl.BlockSpec((tm, tn), lambda i, j: (i, j)),
        out_shape=jax.ShapeDtypeStruct((t, n), F32),
        scratch_shapes=[pltpu.VMEM((tm, d), BF16)],
        compiler_params=_params(2),
        name="inproj",
    )(x2, g, w_bf)


def _lru_kernel(xb_ref, gate_ref, cw_ref, cb_ref, wa_ref, ba_ref, wi_ref, bi_ref, lam_ref,
                o_ref, ext_ref, a_ref, b_ref, hcar_ref):
    tt, c = xb_ref.shape
    hd = c // LRU_HEADS

    @pl.when(pl.program_id(1) == 0)
    def _():
        ext_ref[0:SUBLANES, :] = jnp.zeros((SUBLANES, c), F32)
        hcar_ref[...] = jnp.zeros((1, c), F32)

    x = xb_ref[...]
    ext_ref[SUBLANES:SUBLANES + tt, :] = x
    xc = cb_ref[...] + cw_ref[CONV_WIDTH - 1:CONV_WIDTH, :] * x
    for k in range(1, CONV_WIDTH):
        xc = xc + cw_ref[CONV_WIDTH - 1 - k:CONV_WIDTH - k, :] * ext_ref[SUBLANES - k:SUBLANES - k + tt, :]
    ext_ref[0:SUBLANES, :] = x[tt - SUBLANES:tt, :]

    xcb = xc.astype(BF16)
    ra, ia = [], []
    for h in range(LRU_HEADS):
        xh = xcb[:, h * hd:(h + 1) * hd]
        ra.append(jnp.dot(xh, wa_ref[h], preferred_element_type=F32))
        ia.append(jnp.dot(xh, wi_ref[h], preferred_element_type=F32))
    r = jax.nn.sigmoid(jnp.concatenate(ra, axis=1) + ba_ref[...])
    ig = jax.nn.sigmoid(jnp.concatenate(ia, axis=1) + bi_ref[...])
    log_a = (-LRU_C) * r * _softplus(-lam_ref[...])
    a = jnp.exp(log_a)
    bx = jnp.sqrt(-jnp.tanh(log_a) * (a * a + 1.0)) * (ig * xc)

    row = lax.broadcasted_iota(jnp.int32, (tt, c), 0) % SUBLANES
    for d in (1, 2, 4):
        m = row >= d
        a_sh = jnp.where(m, pltpu.roll(a, d, 0), 1.0)
        b_sh = jnp.where(m, pltpu.roll(bx, d, 0), 0.0)
        bx = a * b_sh + bx
        a = a * a_sh
    a_ref[...] = a
    b_ref[...] = bx

    def blk(i, hc):
        off = pl.multiple_of(i * SUBLANES, SUBLANES)
        hb = a_ref[pl.ds(off, SUBLANES), :] * hc + b_ref[pl.ds(off, SUBLANES), :]
        b_ref[pl.ds(off, SUBLANES), :] = hb
        return hb[SUBLANES - 1:SUBLANES, :]

    hcar_ref[...] = lax.fori_loop(0, tt // SUBLANES, blk, hcar_ref[...])
    o_ref[...] = (b_ref[...] * jax.nn.gelu(gate_ref[...])).astype(BF16)


def _lru(proj, b, s, tt, cw, cb, wa, ba, wi, bi, lam):
    c = cw.shape[1]
    nt = s // tt
    row = lambda bb, ss: (bb * nt + ss, 0)
    const2 = lambda bb, ss: (0, 0)
    const3 = lambda bb, ss: (0, 0, 0)
    hd = c // LRU_HEADS
    return pl.pallas_call(
        _lru_kernel,
        grid=(b, nt),
        in_specs=[pl.BlockSpec((tt, c), row),
                  pl.BlockSpec((tt, c), lambda bb, ss: (bb * nt + ss, 1)),
                  pl.BlockSpec((CONV_WIDTH, c), const2),
                  pl.BlockSpec((1, c), const2),
                  pl.BlockSpec((LRU_HEADS, hd, hd), const3),
                  pl.BlockSpec((1, c), const2),
                  pl.BlockSpec((LRU_HEADS, hd, hd), const3),
                  pl.BlockSpec((1, c), const2),
                  pl.BlockSpec((1, c), const2)],
        out_specs=pl.BlockSpec((tt, c), row),
        out_shape=jax.ShapeDtypeStruct((b * s, c), BF16),
        scratch_shapes=[pltpu.VMEM((tt + SUBLANES, c), F32),
                        pltpu.VMEM((tt, c), F32),
                        pltpu.VMEM((tt, c), F32),
                        pltpu.VMEM((1, c), F32)],
        compiler_params=_params(2),
        name="rg_lru",
    )(proj, proj, cw, cb, wa, ba, wi, bi, lam)


def _token_shift(ext_ref, x, mu):
    tt = x.shape[0]
    ext_ref[SUBLANES:SUBLANES + tt, :] = x
    prev = ext_ref[SUBLANES - 1:SUBLANES - 1 + tt, :]
    ext_ref[0:SUBLANES, :] = x[tt - SUBLANES:tt, :]
    return x + (prev - x) * mu


def _store_head_major(o_ref, q, x):
    n = RWKV_HEAD_DIM
    per = LANES // n
    for cc in range(x.shape[1] // LANES):
        xt = x[:, cc * LANES:(cc + 1) * LANES].T
        for hh in range(per):
            h = per * cc + hh
            o_ref[q, h * N_PITCH:h * N_PITCH + n, :] = xt[hh * n:(hh + 1) * n, :]
            o_ref[q, h * N_PITCH + n:(h + 1) * N_PITCH, :] = jnp.zeros((N_PITCH - n, x.shape[0]), F32)


def _rwkv_prep_kernel(r_ref, k_ref, v_ref, l_ref, mur_ref, muk_ref, muv_ref, mul_ref,
                      w0_ref, w2_ref, a0_ref, a2_ref, g2_ref, kk_ref, ka_ref, rk_ref, e_ref,
                      so_ref, go_ref, bonus_ref,
                      er_ref, ek_ref, ev_ref, el_ref):
    @pl.when(pl.program_id(1) == 0)
    def _():
        for ext in (er_ref, ek_ref, ev_ref, el_ref):
            ext[0:SUBLANES, :] = jnp.zeros((SUBLANES, ext.shape[1]), F32)

    r = _token_shift(er_ref, r_ref[...], mur_ref[...])
    k = _token_shift(ek_ref, k_ref[...], muk_ref[...])
    v = _token_shift(ev_ref, v_ref[...], muv_ref[...])
    lo = _token_shift(el_ref, l_ref[...], mul_ref[...])

    lo_a = lo[:, 0:LANES]
    lo_g = lo[:, LANES:3 * LANES]
    wl = w0_ref[...] + jnp.dot(jnp.tanh(lo_a).astype(BF16), w2_ref[...], preferred_element_type=F32)
    w_log = -_softplus(-wl) - 0.5
    decay = jnp.exp(-jnp.exp(w_log))
    a = jax.nn.sigmoid(a0_ref[...] + jnp.dot(lo_a.astype(BF16), a2_ref[...], preferred_element_type=F32))
    g = jnp.dot(jax.nn.sigmoid(lo_g).astype(BF16), g2_ref[...], preferred_element_type=F32)

    e_blk = e_ref[...]
    kk = k * kk_ref[...]
    kk = kk / jnp.maximum(jnp.sqrt(_segsum(kk * kk, e_blk)), 1e-12)
    kmod = k * (1.0 + (a - 1.0) * ka_ref[...])

    for q, val in enumerate((r, decay, kmod, v, -kk, kk * a)):
        _store_head_major(so_ref, q, val)
    go_ref[...] = g
    bonus_ref[...] = _segsum(r * kmod * rk_ref[...], e_blk) * v


def _rwkv_prep(proj, b, s, tt, c, col0, mur, muk, muv, mul, w0, w2p, a0, a2p, g2p, kk, ka, rk, e_blk):
    nt = s // tt
    cb0 = col0 // c
    lb = (col0 + 3 * c) // LORA_PAD
    const2 = lambda bb, ss: (0, 0)
    vec = pl.BlockSpec((1, c), const2)
    row = lambda bb, ss: (bb * nt + ss, 0)
    out = jax.ShapeDtypeStruct((b * s, c), F32)
    hrows = (c // RWKV_HEAD_DIM) * N_PITCH
    return pl.pallas_call(
        _rwkv_prep_kernel,
        grid=(b, nt),
        in_specs=[pl.BlockSpec((tt, c), lambda bb, ss: (bb * nt + ss, cb0)),
                  pl.BlockSpec((tt, c), lambda bb, ss: (bb * nt + ss, cb0 + 1)),
                  pl.BlockSpec((tt, c), lambda bb, ss: (bb * nt + ss, cb0 + 2)),
                  pl.BlockSpec((tt, LORA_PAD), lambda bb, ss: (bb * nt + ss, lb)),
                  vec, vec, vec, pl.BlockSpec((1, LORA_PAD), const2),
                  vec, pl.BlockSpec((LANES, c), const2),
                  vec, pl.BlockSpec((LANES, c), const2),
                  pl.BlockSpec((2 * LANES, c), const2),
                  vec, vec, vec, pl.BlockSpec((LANES, LANES), const2)],
        out_specs=[pl.BlockSpec((N_SCAN_IN, hrows, tt), lambda bb, ss: (0, bb, ss)),
                   pl.BlockSpec((tt, c), row), pl.BlockSpec((tt, c), row)],
        out_shape=[jax.ShapeDtypeStruct((N_SCAN_IN, b * hrows, s), F32), out, out],
        scratch_shapes=[pltpu.VMEM((tt + SUBLANES, c), F32),
                        pltpu.VMEM((tt + SUBLANES, c), F32),
                        pltpu.VMEM((tt + SUBLANES, c), F32),
                        pltpu.VMEM((tt + SUBLANES, LORA_PAD), F32)],
        compiler_params=_params(2),
        name="rwkv_prep",
    )(proj, proj, proj, proj, mur, muk, muv, mul, w0, w2p, a0, a2p, g2p, kk, ka, rk, e_blk)


def _swap_kernel(i_ref, o_ref):
    for n in range(RWKV_HEAD_DIM):
        o_ref[pl.ds(n, LANES, stride=N_PITCH), :] = i_ref[pl.ds(n, LANES, stride=N_PITCH), :].T
    for n in range(RWKV_HEAD_DIM, N_PITCH):
        o_ref[pl.ds(n, LANES, stride=N_PITCH), :] = jnp.zeros((LANES, LANES), F32)


def _to_time_major(p):
    nq, rows, s = p.shape
    return pl.pallas_call(
        _swap_kernel,
        grid=(nq, s // LANES),
        in_specs=[pl.BlockSpec((None, rows, LANES), lambda q, i: (q, 0, i))],
        out_specs=pl.BlockSpec((None, rows, LANES), lambda q, i: (q, i, 0)),
        out_shape=jax.ShapeDtypeStruct((nq, s * N_PITCH, LANES), F32),
        compiler_params=_params(2),
        name="to_time_major",
    )(p)


def _to_head_major(y):
    rows_total, bh = y.shape
    s = rows_total // N_PITCH
    rows = LANES * N_PITCH
    return pl.pallas_call(
        _swap_kernel,
        grid=(s // LANES,),
        in_specs=[pl.BlockSpec((rows, LANES), lambda i: (i, 0))],
        out_specs=pl.BlockSpec((rows, LANES), lambda i: (0, i)),
        out_shape=jax.ShapeDtypeStruct((bh * N_PITCH, s), F32),
        compiler_params=_params(1),
        name="to_head_major",
    )(y)


def _rwkv_scan_kernel(q_ref, y_ref, s_ref, sa_ref):
    n = RWKV_HEAD_DIM
    tt = y_ref.shape[0] // N_PITCH

    @pl.when(pl.program_id(0) == 0)
    def _():
        s_ref[...] = jnp.zeros(s_ref.shape, F32)

    def step(t, carry):
        base = pl.multiple_of(t * N_PITCH, SUBLANES)
        r = q_ref[0, pl.ds(base, n), :]
        w = q_ref[1, pl.ds(base, n), :]
        k = q_ref[2, pl.ds(base, n), :]
        na = q_ref[4, pl.ds(base, n), :]
        b = q_ref[5, pl.ds(base, n), :]
        for i in range(n):
            sa_ref[i:i + 1, :] = jnp.sum(s_ref[i] * na, axis=0, keepdims=True)
        for i in range(n):
            s_new = s_ref[i] * w + sa_ref[i:i + 1, :] * b + q_ref[3, pl.ds(base + i, 1), :] * k
            s_ref[i] = s_new
            y_ref[pl.ds(base + i, 1), :] = jnp.sum(s_new * r, axis=0, keepdims=True)
        y = y_ref[pl.ds(base, n), :]
        mean = jnp.mean(y, axis=0, keepdims=True)
        yc = y - mean
        var = jnp.mean(yc * yc, axis=0, keepdims=True)
        y_ref[pl.ds(base, n), :] = yc * lax.rsqrt(var + GN_EPS)
        y_ref[pl.ds(base + n, N_PITCH - n), :] = jnp.zeros((N_PITCH - n, y_ref.shape[1]), F32)
        return carry

    lax.fori_loop(0, tt, step, 0)


def _rwkv_scan(q, tt):
    nq, rows_total, bh = q.shape
    s = rows_total // N_PITCH
    n = RWKV_HEAD_DIM
    return pl.pallas_call(
        _rwkv_scan_kernel,
        grid=(s // tt,),
        in_specs=[pl.BlockSpec((nq, tt * N_PITCH, bh), lambda i: (0, i, 0))],
        out_specs=pl.BlockSpec((tt * N_PITCH, bh), lambda i: (i, 0)),
        out_shape=jax.ShapeDtypeStruct((rows_total, bh), F32),
        scratch_shapes=[pltpu.VMEM((n, n, bh), F32), pltpu.VMEM((n, bh), F32)],
        compiler_params=_params(1),
        name="rwkv_scan",
    )(q)


def _outproj_kernel(x_ref, ylru_ref, yn_ref, bonus_ref, g_ref, lw_ref, lb_ref, wo1_ref, wo2_ref,
                    n2_ref, x1_ref, h2_ref):
    n = RWKV_HEAD_DIM
    per = LANES // n
    cols = []
    for cc in range(bonus_ref.shape[1] // LANES):
        blk = jnp.concatenate([yn_ref[(per * cc + hh) * N_PITCH:(per * cc + hh) * N_PITCH + n, :]
                               for hh in range(per)], axis=0)
        cols.append(blk.T)
    yn = jnp.concatenate(cols, axis=1)
    y_rwkv = ((yn * lw_ref[...] + lb_ref[...] + bonus_ref[...]) * g_ref[...]).astype(BF16)
    x1 = (x_ref[...]
          + jnp.dot(ylru_ref[...], wo1_ref[...], preferred_element_type=F32)
          + jnp.dot(y_rwkv, wo2_ref[...], preferred_element_type=F32))
    x1_ref[...] = x1
    ms = jnp.mean(x1 * x1, axis=-1, keepdims=True)
    h2_ref[...] = (x1 * lax.rsqrt(ms + NORM_EPS) * n2_ref[...]).astype(BF16)


def _outproj(x2, ylru, yn, bonus, g, lw, lb, wo1, wo2, n2, tm):
    t, d = x2.shape
    c = ylru.shape[1]
    s = yn.shape[1]
    nst = s // tm
    hrows = (c // RWKV_HEAD_DIM) * N_PITCH
    row_d = pl.BlockSpec((tm, d), lambda i: (i, 0))
    row_c = pl.BlockSpec((tm, c), lambda i: (i, 0))
    const = lambda i: (0, 0)
    return pl.pallas_call(
        _outproj_kernel,
        grid=(t // tm,),
        in_specs=[row_d, row_c, pl.BlockSpec((hrows, tm), lambda i: (i // nst, i % nst)), row_c, row_c,
                  pl.BlockSpec((1, c), const), pl.BlockSpec((1, c), const),
                  pl.BlockSpec((c, d), const), pl.BlockSpec((c, d), const),
                  pl.BlockSpec((1, d), const)],
        out_specs=[row_d, row_d],
        out_shape=[jax.ShapeDtypeStruct((t, d), F32), jax.ShapeDtypeStruct((t, d), BF16)],
        compiler_params=_params(1),
        name="outproj",
    )(x2, ylru, yn, bonus, g, lw, lb, wo1, wo2, n2)


def _matmul_kernel(a_ref, w_ref, o_ref):
    o_ref[...] = jnp.dot(a_ref[...], w_ref[...], preferred_element_type=F32)


def _qproj(h2, wq, tm):
    t, d = h2.shape
    n = wq.shape[1]
    return pl.pallas_call(
        _matmul_kernel,
        grid=(t // tm,),
        in_specs=[pl.BlockSpec((tm, d), lambda i: (i, 0)),
                  pl.BlockSpec((d, n), lambda i: (0, 0))],
        out_specs=pl.BlockSpec((tm, n), lambda i: (i, 0)),
        out_shape=jax.ShapeDtypeStruct((t, n), F32),
        compiler_params=_params(1),
        name="qproj",
    )(h2, wq)


def _dot_nt3(a, b):
    dn = (((1,), (1,)), ((), ()))
    ah, al = _split_bf16(a)
    bh, bl = _split_bf16(b)
    return (lax.dot_general(ah, bh, dn, preferred_element_type=F32)
            + lax.dot_general(ah, bl, dn, preferred_element_type=F32)
            + lax.dot_general(al, bh, dn, preferred_element_type=F32))


def _tree(op, xs):
    xs = list(xs)
    while len(xs) > 1:
        nxt = [op(xs[i], xs[i + 1]) for i in range(0, len(xs) - 1, 2)]
        if len(xs) % 2:
            nxt.append(xs[-1])
        xs = nxt
    return xs[0]


def _route_kernel(q_ref, keys_ref, g_ref, ii_ref, jj_ref):
    tm = q_ref.shape[0]
    nk = PEER_NKEYS
    neg = -jnp.inf
    iota_n = lax.broadcasted_iota(jnp.int32, (nk, tm), 0).astype(F32)

    vals = [[None, None] for _ in range(PEER_HEADS)]
    idxs = [[None, None] for _ in range(PEER_HEADS)]
    for h in range(PEER_HEADS):
        for p in range(2):
            hp = 2 * h + p
            cur = _dot_nt3(keys_ref[hp], q_ref[:, hp * nk:(hp + 1) * nk])
            vs, ix = [], []
            for _ in range(PEER_TOPK):
                m = jnp.max(cur, axis=0, keepdims=True)
                idx = jnp.min(jnp.where(cur == m, iota_n, float(nk)), axis=0, keepdims=True)
                vs.append(m)
                ix.append(idx)
                cur = jnp.where(iota_n == idx, neg, cur)
            vals[h][p] = vs
            idxs[h][p] = ix

    def by_rank(table, p, rnk):
        return jnp.concatenate([table[h][p][rnk] for h in range(PEER_HEADS)], axis=0)

    va = [by_rank(vals, 0, r) for r in range(PEER_TOPK)]
    vb = [by_rank(vals, 1, r) for r in range(PEER_TOPK)]
    ia = [by_rank(idxs, 0, r) for r in range(PEER_TOPK)]
    ib = [by_rank(idxs, 1, r) for r in range(PEER_TOPK)]

    cand = []
    for p in range(PEER_TOPK):
        for qq in range(PEER_TOPK // (p + 1)):
            cand.append([va[p] + vb[qq], ia[p], ib[qq]])
    ncand = len(cand)

    best, sel_i, sel_j = [], [], []
    for _ in range(PEER_TOPK):
        m = _tree(jnp.maximum, [c[0] for c in cand])
        code = _tree(jnp.minimum, [jnp.where(c[0] == m, float(ci), float(ncand))
                                   for ci, c in enumerate(cand)])
        ci_sel = jnp.zeros_like(m)
        cj_sel = jnp.zeros_like(m)
        for ci, c in enumerate(cand):
            win = code == float(ci)
            ci_sel = jnp.where(win, c[1], ci_sel)
            cj_sel = jnp.where(win, c[2], cj_sel)
            c[0] = jnp.where(win, neg, c[0])
        best.append(m)
        sel_i.append(ci_sel)
        sel_j.append(cj_sel)

    ex = [jnp.exp(bv - best[0]) for bv in best]
    z = _tree(jnp.add, ex)
    gates = jnp.concatenate([e / z for e in ex], axis=0)
    g_ref[...] = gates.T
    ii_ref[...] = jnp.concatenate(sel_i, axis=0).T
    jj_ref[...] = jnp.concatenate(sel_j, axis=0).T


def _route(q, keys, tm):
    t, dq = q.shape
    nslot = PEER_TOPK * PEER_HEADS
    out = jax.ShapeDtypeStruct((t, nslot), F32)
    oblk = pl.BlockSpec((tm, nslot), lambda i: (i, 0))
    return pl.pallas_call(
        _route_kernel,
        grid=(t // tm,),
        in_specs=[pl.BlockSpec((tm, dq), lambda i: (i, 0)),
                  pl.BlockSpec(keys.shape, lambda i: (0, 0, 0))],
        out_specs=[oblk, oblk, oblk],
        out_shape=[out, out, out],
        compiler_params=_params(1),
        name="peer_route",
    )(q, keys)


def _wbuild_kernel(g_ref, ii_ref, jj_ref, w_ref, buf_ref):
    tw = g_ref.shape[0]
    nk = PEER_NKEYS
    iota_s = lax.broadcasted_iota(jnp.int32, (nk, g_ref.shape[1]), 0).astype(F32)
    dn = (((1,), (1,)), ((), ()))

    def tok(t, carry):
        g = g_ref[pl.ds(t, 1), :]
        a = jnp.where(iota_s == ii_ref[pl.ds(t, 1), :], g, 0.0).astype(BF16)
        bt = jnp.where(iota_s == jj_ref[pl.ds(t, 1), :], 1.0, 0.0).astype(BF16)
        off = pl.multiple_of(t * W_PITCH, SUBLANES)
        buf_ref[pl.ds(off, nk), :] = lax.dot_general(a, bt, dn, preferred_element_type=F32)
        return carry

    lax.fori_loop(0, tw, tok, 0, unroll=TOKEN_UNROLL)
    for i in range(nk):
        w_ref[:, i * nk:(i + 1) * nk] = buf_ref[pl.ds(i, tw, stride=W_PITCH), :].astype(BF16)


def _wbuild(gates, ii, jj, tw):
    t, nslot = gates.shape
    ne = PEER_NKEYS * PEER_NKEYS
    blk = pl.BlockSpec((tw, nslot), lambda i: (i, 0))
    return pl.pallas_call(
        _wbuild_kernel,
        grid=(t // tw,),
        in_specs=[blk, blk, blk],
        out_specs=pl.BlockSpec((tw, ne), lambda i: (i, 0)),
        out_shape=jax.ShapeDtypeStruct((t, ne), BF16),
        scratch_shapes=[pltpu.VMEM((tw * W_PITCH, PEER_NKEYS), F32)],
        compiler_params=_params(1),
        name="peer_weights",
    )(gates, ii, jj)


def _peer_kernel(h_ref, u_ref, v_ref, w_ref, o_ref):
    @pl.when(pl.program_id(1) == 0)
    def _():
        o_ref[...] = jnp.zeros(o_ref.shape, F32)

    act = lax.dot_general(h_ref[...], u_ref[...], (((1,), (1,)), ((), ())),
                          preferred_element_type=F32)
    p = (jax.nn.gelu(act) * w_ref[...].astype(F32)).astype(BF16)
    o_ref[...] += jnp.dot(p, v_ref[...], preferred_element_type=F32)


def _peer(h2, u_bf, v_bf, w, tm, te):
    t, d = h2.shape
    ne = u_bf.shape[0]
    return pl.pallas_call(
        _peer_kernel,
        grid=(t // tm, ne // te),
        in_specs=[pl.BlockSpec((tm, d), lambda i, e: (i, 0)),
                  pl.BlockSpec((te, d), lambda i, e: (e, 0)),
                  pl.BlockSpec((te, d), lambda i, e: (e, 0)),
                  pl.BlockSpec((tm, te), lambda i, e: (i, e))],
        out_specs=pl.BlockSpec((tm, d), lambda i, e: (i, 0)),
        out_shape=jax.ShapeDtypeStruct((t, d), F32),
        compiler_params=_params(2),
        name="peer_experts",
    )(h2, u_bf, v_bf, w)


def _final_kernel(x1_ref, p_ref, g_ref, o_ref):
    x = x1_ref[...] + p_ref[...]
    ms = jnp.mean(x * x, axis=-1, keepdims=True)
    o_ref[...] = x * lax.rsqrt(ms + NORM_EPS) * g_ref[...]


def _final(x1, peer, g, tm):
    t, d = x1.shape
    row = pl.BlockSpec((tm, d), lambda i: (i, 0))
    return pl.pallas_call(
        _final_kernel,
        grid=(t // tm,),
        in_specs=[row, row, pl.BlockSpec((1, d), lambda i: (0, 0))],
        out_specs=row,
        out_shape=jax.ShapeDtypeStruct((t, d), F32),
        compiler_params=_params(1),
        name="final_norm",
    )(x1, peer, g)


def _tiles(t, s):
    return dict(
        inproj_tm=min(1024, t), inproj_tn=512,
        seq_tt=min(256, s),
        scan_tt=min(16, s),
        outproj_tm=min(256, s),
        qproj_tm=min(512, t),
        route_tm=min(128, t),
        wbuild_tw=min(128, t),
        peer_tm=min(1024, t), peer_te=512,
        final_tm=min(512, t),
    )


def _layer(x2, b, s, norm1_g, w_in, conv_w, conv_b, lru_wa, lru_ba, lru_wi, lru_bi, lru_lambda,
           rwkv_mu, rwkv_w0, rwkv_w2, rwkv_a0, rwkv_a2, rwkv_g2, rwkv_k_k, rwkv_k_a, rwkv_r_k,
           rwkv_lnx_w, rwkv_lnx_b, w_out, norm2_g, peer_wq, peer_keys, peer_u, peer_v):
    t, d = x2.shape
    c_lru = conv_w.shape[1]
    c_rwkv = rwkv_w0.shape[0]
    n_heads = c_rwkv // RWKV_HEAD_DIM
    tl = _tiles(t, s)
    row = lambda v: v.reshape(1, -1)

    n_main = 2 * c_lru + 3 * c_rwkv
    n_lora = DECAY_LORA + AAA_LORA + GATE_LORA
    w_in_p = jnp.pad(w_in, ((0, 0), (0, LORA_PAD - n_lora))).astype(BF16)
    proj = _inproj(x2, row(norm1_g), w_in_p, tl["inproj_tm"], tl["inproj_tn"])

    y_lru = _lru(proj, b, s, tl["seq_tt"], conv_w, row(conv_b), lru_wa.astype(BF16), row(lru_ba),
                 lru_wi.astype(BF16), row(lru_bi), row(lru_lambda))

    mu_main, mu_lora = rwkv_mu[:3 * c_rwkv], rwkv_mu[3 * c_rwkv:]
    mu_l = jnp.pad(mu_lora, (0, LORA_PAD - n_lora))
    w2p = jnp.pad(rwkv_w2, ((0, LANES - DECAY_LORA), (0, 0))).astype(BF16)
    a2p = jnp.pad(rwkv_a2, ((DECAY_LORA, LANES - DECAY_LORA - AAA_LORA), (0, 0))).astype(BF16)
    g2p = jnp.pad(rwkv_g2, ((0, 2 * LANES - GATE_LORA), (0, 0))).astype(BF16)
    seg = np.arange(LANES) // RWKV_HEAD_DIM
    e_blk = jnp.asarray(seg[:, None] == seg[None, :], dtype=BF16)
    assert b * n_heads == LANES, "the scan maps (batch, head) onto the 128 lanes"
    scan_in, g_s, bonus = _rwkv_prep(
        proj, b, s, tl["seq_tt"], c_rwkv, 2 * c_lru,
        row(mu_main[:c_rwkv]), row(mu_main[c_rwkv:2 * c_rwkv]), row(mu_main[2 * c_rwkv:]), row(mu_l),
        row(rwkv_w0), w2p, row(rwkv_a0), a2p, g2p, row(rwkv_k_k), row(rwkv_k_a),
        row(rwkv_r_k), e_blk)

    yn = _to_head_major(_rwkv_scan(_to_time_major(scan_in), tl["scan_tt"]))

    w_out_bf = w_out.astype(BF16)
    x1, h2 = _outproj(x2, y_lru, yn, bonus, g_s, row(rwkv_lnx_w), row(rwkv_lnx_b),
                      w_out_bf[:c_lru], w_out_bf[c_lru:], row(norm2_g), tl["outproj_tm"])

    q = _qproj(h2, peer_wq.astype(BF16), tl["qproj_tm"])
    keys = peer_keys.reshape(2 * PEER_HEADS, PEER_NKEYS, -1)
    gates, ii, jj = _route(q, keys, tl["route_tm"])
    w = _wbuild(gates, ii, jj, tl["wbuild_tw"])
    peer = _peer(h2, peer_u.astype(BF16), peer_v.astype(BF16), w, tl["peer_tm"], tl["peer_te"])
    return x1, peer


def kernel(x, norm1_g, w_in, conv_w, conv_b, lru_wa, lru_ba, lru_wi, lru_bi, lru_lambda, rwkv_mu, rwkv_w0, rwkv_w2, rwkv_a0, rwkv_a2, rwkv_g2, rwkv_k_k, rwkv_k_a, rwkv_r_k, rwkv_lnx_w, rwkv_lnx_b, w_out, norm2_g, peer_wq, peer_keys, peer_u, peer_v, normf_g):
    b, s, d = x.shape
    depth = norm1_g.shape[0]
    x2 = x.reshape(b * s, d)
    peer = jnp.zeros_like(x2)
    for l in range(depth):
        if l > 0:
            x2 = x2 + peer
        x2, peer = _layer(x2, b, s, norm1_g[l], w_in[l], conv_w[l], conv_b[l], lru_wa[l], lru_ba[l],
                          lru_wi[l], lru_bi[l], lru_lambda[l], rwkv_mu[l], rwkv_w0[l], rwkv_w2[l],
                          rwkv_a0[l], rwkv_a2[l], rwkv_g2[l], rwkv_k_k[l], rwkv_k_a[l],
                          rwkv_r_k[l].reshape(-1), rwkv_lnx_w[l], rwkv_lnx_b[l], w_out[l], norm2_g[l],
                          peer_wq[l], peer_keys[l], peer_u[l], peer_v[l])
    out = _final(x2, peer, normf_g.reshape(1, -1), _tiles(b * s, s)["final_tm"])
    return out.reshape(b, s, d)
```

```python
import functools

import jax
import jax.numpy as jnp
import numpy as np
from jax import lax
from jax.experimental import pallas as pl
from jax.experimental.pallas import tpu as pltpu

NORM_EPS = 1e-6
GN_EPS = 64e-5
LRU_C = 8.0
LRU_HEADS = 8
CONV_WIDTH = 4
RWKV_HEAD_DIM = 64
DECAY_LORA = 64
AAA_LORA = 64
GATE_LORA = 160
PEER_HEADS = 8
PEER_NKEYS = 128
PEER_TOPK = 16

LANES = 128
SUBLANES = 8
LORA_PAD = 512
W_PITCH = 136
N_PITCH = 72
N_SCAN_IN = 6
PEER_ROW_CHUNKS = 4
TOKEN_UNROLL = 8
VMEM_LIMIT = 48 * 1024 * 1024

F32 = jnp.float32
BF16 = jnp.bfloat16
FP8 = jnp.float8_e4m3fn
FP8_TARGET = 256.0
TINY = 1e-30


def _fp8_scale(amax):
    return jnp.maximum(amax, TINY) * (1.0 / FP8_TARGET)


def _params(n_axes):
    return pltpu.CompilerParams(dimension_semantics=("arbitrary",) * n_axes,
                                vmem_limit_bytes=VMEM_LIMIT)


def _softplus(z):
    return jnp.maximum(z, 0.0) + jnp.log1p(jnp.exp(-jnp.abs(z)))


def _split_bf16(x):
    hi = x.astype(BF16)
    lo = (x - hi.astype(F32)).astype(BF16)
    return hi, lo


def _segsum(x, e_blk):
    outs = []
    for c in range(x.shape[1] // LANES):
        hi, lo = _split_bf16(x[:, c * LANES:(c + 1) * LANES])
        outs.append(jnp.dot(hi, e_blk, preferred_element_type=F32)
                    + jnp.dot(lo, e_blk, preferred_element_type=F32))
    return jnp.concatenate(outs, axis=1)


def _inproj_kernel(x_ref, g_ref, w_ref, o_ref, h_ref):
    @pl.when(pl.program_id(1) == 0)
    def _():
        x = x_ref[...]
        ms = jnp.mean(x * x, axis=-1, keepdims=True)
        h_ref[...] = (x * lax.rsqrt(ms + NORM_EPS) * g_ref[...]).astype(BF16)

    o_ref[...] = jnp.dot(h_ref[...], w_ref[...], preferred_element_type=F32)


def _inproj(x2, g, w_bf, tm, tn):
    t, d = x2.shape
    n = w_bf.shape[1]
    return pl.pallas_call(
        _inproj_kernel,
        grid=(t // tm, n // tn),
        in_specs=[pl.BlockSpec((tm, d), lambda i, j: (i, 0)),
                  pl.BlockSpec((1, d), lambda i, j: (0, 0)),
                  pl.BlockSpec((d, tn), lambda i, j: (0, j))],
        out_specs=pl.BlockSpec((tm, tn), lambda i, j: (i, j)),
        out_shape=jax.ShapeDtypeStruct((t, n), F32),
        scratch_shapes=[pltpu.VMEM((tm, d), BF16)],
        compiler_params=_params(2),
        name="inproj",
    )(x2, g, w_bf)


def _lru_kernel(xb_ref, gate_ref, cw_ref, cb_ref, wa_ref, ba_ref, wi_ref, bi_ref, lam_ref,
                o_ref, ext_ref, a_ref, b_ref, hcar_ref):
    tt, c = xb_ref.shape
    hd = c // LRU_HEADS

    @pl.when(pl.program_id(1) == 0)
    def _():
        ext_ref[0:SUBLANES, :] = jnp.zeros((SUBLANES, c), F32)
        hcar_ref[...] = jnp.zeros((1, c), F32)

    x = xb_ref[...]
    ext_ref[SUBLANES:SUBLANES + tt, :] = x
    xc = cb_ref[...] + cw_ref[CONV_WIDTH - 1:CONV_WIDTH, :] * x
    for k in range(1, CONV_WIDTH):
        xc = xc + cw_ref[CONV_WIDTH - 1 - k:CONV_WIDTH - k, :] * ext_ref[SUBLANES - k:SUBLANES - k + tt, :]
    ext_ref[0:SUBLANES, :] = x[tt - SUBLANES:tt, :]

    xcb = xc.astype(BF16)
    ra, ia = [], []
    for h in range(LRU_HEADS):
        xh = xcb[:, h * hd:(h + 1) * hd]
        ra.append(jnp.dot(xh, wa_ref[h], preferred_element_type=F32))
        ia.append(jnp.dot(xh, wi_ref[h], preferred_element_type=F32))
    r = jax.nn.sigmoid(jnp.concatenate(ra, axis=1) + ba_ref[...])
    ig = jax.nn.sigmoid(jnp.concatenate(ia, axis=1) + bi_ref[...])
    log_a = (-LRU_C) * r * _softplus(-lam_ref[...])
    a = jnp.exp(log_a)
    bx = jnp.sqrt(-jnp.tanh(log_a) * (a * a + 1.0)) * (ig * xc)

    row = lax.broadcasted_iota(jnp.int32, (tt, c), 0) % SUBLANES
    for d in (1, 2, 4):
        m = row >= d
        a_sh = jnp.where(m, pltpu.roll(a, d, 0), 1.0)
        b_sh = jnp.where(m, pltpu.roll(bx, d, 0), 0.0)
        bx = a * b_sh + bx
        a = a * a_sh
    a_ref[...] = a
    b_ref[...] = bx

    def blk(i, hc):
        off = pl.multiple_of(i * SUBLANES, SUBLANES)
        hb = a_ref[pl.ds(off, SUBLANES), :] * hc + b_ref[pl.ds(off, SUBLANES), :]
        b_ref[pl.ds(off, SUBLANES), :] = hb
        return hb[SUBLANES - 1:SUBLANES, :]

    hcar_ref[...] = lax.fori_loop(0, tt // SUBLANES, blk, hcar_ref[...])
    o_ref[...] = (b_ref[...] * jax.nn.gelu(gate_ref[...])).astype(BF16)


def _lru(proj, b, s, tt, cw, cb, wa, ba, wi, bi, lam):
    c = cw.shape[1]
    nt = s // tt
    row = lambda bb, ss: (bb * nt + ss, 0)
    const2 = lambda bb, ss: (0, 0)
    const3 = lambda bb, ss: (0, 0, 0)
    hd = c // LRU_HEADS
    return pl.pallas_call(
        _lru_kernel,
        grid=(b, nt),
        in_specs=[pl.BlockSpec((tt, c), row),
                  pl.BlockSpec((tt, c), lambda bb, ss: (bb * nt + ss, 1)),
                  pl.BlockSpec((CONV_WIDTH, c), const2),
                  pl.BlockSpec((1, c), const2),
                  pl.BlockSpec((LRU_HEADS, hd, hd), const3),
                  pl.BlockSpec((1, c), const2),
                  pl.BlockSpec((LRU_HEADS, hd, hd), const3),
                  pl.BlockSpec((1, c), const2),
                  pl.BlockSpec((1, c), const2)],
        out_specs=pl.BlockSpec((tt, c), row),
        out_shape=jax.ShapeDtypeStruct((b * s, c), BF16),
        scratch_shapes=[pltpu.VMEM((tt + SUBLANES, c), F32),
                        pltpu.VMEM((tt, c), F32),
                        pltpu.VMEM((tt, c), F32),
                        pltpu.VMEM((1, c), F32)],
        compiler_params=_params(2),
        name="rg_lru",
    )(proj, proj, cw, cb, wa, ba, wi, bi, lam)


def _token_shift(ext_ref, x, mu):
    tt = x.shape[0]
    ext_ref[SUBLANES:SUBLANES + tt, :] = x
    prev = ext_ref[SUBLANES - 1:SUBLANES - 1 + tt, :]
    ext_ref[0:SUBLANES, :] = x[tt - SUBLANES:tt, :]
    return x + (prev - x) * mu


def _store_head_major(o_ref, q, x):
    n = RWKV_HEAD_DIM
    per = LANES // n
    for cc in range(x.shape[1] // LANES):
        xt = x[:, cc * LANES:(cc + 1) * LANES].T
        for hh in range(per):
            h = per * cc + hh
            o_ref[q, h * N_PITCH:h * N_PITCH + n, :] = xt[hh * n:(hh + 1) * n, :]
            o_ref[q, h * N_PITCH + n:(h + 1) * N_PITCH, :] = jnp.zeros((N_PITCH - n, x.shape[0]), F32)


def _rwkv_prep_kernel(r_ref, k_ref, v_ref, l_ref, mur_ref, muk_ref, muv_ref, mul_ref,
                      w0_ref, w2_ref, a0_ref, a2_ref, g2_ref, kk_ref, ka_ref, rk_ref, e_ref,
                      so_ref, go_ref, bonus_ref,
                      er_ref, ek_ref, ev_ref, el_ref):
    @pl.when(pl.program_id(1) == 0)
    def _():
        for ext in (er_ref, ek_ref, ev_ref, el_ref):
            ext[0:SUBLANES, :] = jnp.zeros((SUBLANES, ext.shape[1]), F32)

    r = _token_shift(er_ref, r_ref[...], mur_ref[...])
    k = _token_shift(ek_ref, k_ref[...], muk_ref[...])
    v = _token_shift(ev_ref, v_ref[...], muv_ref[...])
    lo = _token_shift(el_ref, l_ref[...], mul_ref[...])

    lo_a = lo[:, 0:LANES]
    lo_g = lo[:, LANES:3 * LANES]
    wl = w0_ref[...] + jnp.dot(jnp.tanh(lo_a).astype(BF16), w2_ref[...], preferred_element_type=F32)
    w_log = -_softplus(-wl) - 0.5
    decay = jnp.exp(-jnp.exp(w_log))
    a = jax.nn.sigmoid(a0_ref[...] + jnp.dot(lo_a.astype(BF16), a2_ref[...], preferred_element_type=F32))
    g = jnp.dot(jax.nn.sigmoid(lo_g).astype(BF16), g2_ref[...], preferred_element_type=F32)

    e_blk = e_ref[...]
    kk = k * kk_ref[...]
    kk = kk / jnp.maximum(jnp.sqrt(_segsum(kk * kk, e_blk)), 1e-12)
    kmod = k * (1.0 + (a - 1.0) * ka_ref[...])

    for q, val in enumerate((r, decay, kmod, v, -kk, kk * a)):
        _store_head_major(so_ref, q, val)
    go_ref[...] = g
    bonus_ref[...] = _segsum(r * kmod * rk_ref[...], e_blk) * v


def _rwkv_prep(proj, b, s, tt, c, col0, mur, muk, muv, mul, w0, w2p, a0, a2p, g2p, kk, ka, rk, e_blk):
    nt = s // tt
    cb0 = col0 // c
    lb = (col0 + 3 * c) // LORA_PAD
    const2 = lambda bb, ss: (0, 0)
    vec = pl.BlockSpec((1, c), const2)
    row = lambda bb, ss: (bb * nt + ss, 0)
    out = jax.ShapeDtypeStruct((b * s, c), F32)
    hrows = (c // RWKV_HEAD_DIM) * N_PITCH
    return pl.pallas_call(
        _rwkv_prep_kernel,
        grid=(b, nt),
        in_specs=[pl.BlockSpec((tt, c), lambda bb, ss: (bb * nt + ss, cb0)),
                  pl.BlockSpec((tt, c), lambda bb, ss: (bb * nt + ss, cb0 + 1)),
                  pl.BlockSpec((tt, c), lambda bb, ss: (bb * nt + ss, cb0 + 2)),
                  pl.BlockSpec((tt, LORA_PAD), lambda bb, ss: (bb * nt + ss, lb)),
                  vec, vec, vec, pl.BlockSpec((1, LORA_PAD), const2),
                  vec, pl.BlockSpec((LANES, c), const2),
                  vec, pl.BlockSpec((LANES, c), const2),
                  pl.BlockSpec((2 * LANES, c), const2),
                  vec, vec, vec, pl.BlockSpec((LANES, LANES), const2)],
        out_specs=[pl.BlockSpec((N_SCAN_IN, hrows, tt), lambda bb, ss: (0, bb, ss)),
                   pl.BlockSpec((tt, c), row), pl.BlockSpec((tt, c), row)],
        out_shape=[jax.ShapeDtypeStruct((N_SCAN_IN, b * hrows, s), F32), out, out],
        scratch_shapes=[pltpu.VMEM((tt + SUBLANES, c), F32),
                        pltpu.VMEM((tt + SUBLANES, c), F32),
                        pltpu.VMEM((tt + SUBLANES, c), F32),
                        pltpu.VMEM((tt + SUBLANES, LORA_PAD), F32)],
        compiler_params=_params(2),
        name="rwkv_prep",
    )(proj, proj, proj, proj, mur, muk, muv, mul, w0, w2p, a0, a2p, g2p, kk, ka, rk, e_blk)


def _swap_kernel(i_ref, o_ref):
    for n in range(RWKV_HEAD_DIM):
        o_ref[pl.ds(n, LANES, stride=N_PITCH), :] = i_ref[pl.ds(n, LANES, stride=N_PITCH), :].T
    for n in range(RWKV_HEAD_DIM, N_PITCH):
        o_ref[pl.ds(n, LANES, stride=N_PITCH), :] = jnp.zeros((LANES, LANES), F32)


def _to_time_major(p):
    nq, rows, s = p.shape
    return pl.pallas_call(
        _swap_kernel,
        grid=(nq, s // LANES),
        in_specs=[pl.BlockSpec((None, rows, LANES), lambda q, i: (q, 0, i))],
        out_specs=pl.BlockSpec((None, rows, LANES), lambda q, i: (q, i, 0)),
        out_shape=jax.ShapeDtypeStruct((nq, s * N_PITCH, LANES), F32),
        compiler_params=_params(2),
        name="to_time_major",
    )(p)


def _to_head_major(y):
    rows_total, bh = y.shape
    s = rows_total // N_PITCH
    rows = LANES * N_PITCH
    return pl.pallas_call(
        _swap_kernel,
        grid=(s // LANES,),
        in_specs=[pl.BlockSpec((rows, LANES), lambda i: (i, 0))],
        out_specs=pl.BlockSpec((rows, LANES), lambda i: (0, i)),
        out_shape=jax.ShapeDtypeStruct((bh * N_PITCH, s), F32),
        compiler_params=_params(1),
        name="to_head_major",
    )(y)


def _rwkv_scan_kernel(q_ref, y_ref, s_ref, sa_ref):
    n = RWKV_HEAD_DIM
    tt = y_ref.shape[0] // N_PITCH

    @pl.when(pl.program_id(0) == 0)
    def _():
        s_ref[...] = jnp.zeros(s_ref.shape, F32)

    def step(t, carry):
        base = pl.multiple_of(t * N_PITCH, SUBLANES)
        r = q_ref[0, pl.ds(base, n), :]
        w = q_ref[1, pl.ds(base, n), :]
        k = q_ref[2, pl.ds(base, n), :]
        na = q_ref[4, pl.ds(base, n), :]
        b = q_ref[5, pl.ds(base, n), :]
        for i in range(n):
            sa_ref[i:i + 1, :] = jnp.sum(s_ref[i] * na, axis=0, keepdims=True)
        for i in range(n):
            s_new = s_ref[i] * w + sa_ref[i:i + 1, :] * b + q_ref[3, pl.ds(base + i, 1), :] * k
            s_ref[i] = s_new
            y_ref[pl.ds(base + i, 1), :] = jnp.sum(s_new * r, axis=0, keepdims=True)
        y = y_ref[pl.ds(base, n), :]
        mean = jnp.mean(y, axis=0, keepdims=True)
        yc = y - mean
        var = jnp.mean(yc * yc, axis=0, keepdims=True)
        y_ref[pl.ds(base, n), :] = yc * lax.rsqrt(var + GN_EPS)
        y_ref[pl.ds(base + n, N_PITCH - n), :] = jnp.zeros((N_PITCH - n, y_ref.shape[1]), F32)
        return carry

    lax.fori_loop(0, tt, step, 0)


def _rwkv_scan(q, tt):
    nq, rows_total, bh = q.shape
    s = rows_total // N_PITCH
    n = RWKV_HEAD_DIM
    return pl.pallas_call(
        _rwkv_scan_kernel,
        grid=(s // tt,),
        in_specs=[pl.BlockSpec((nq, tt * N_PITCH, bh), lambda i: (0, i, 0))],
        out_specs=pl.BlockSpec((tt * N_PITCH, bh), lambda i: (i, 0)),
        out_shape=jax.ShapeDtypeStruct((rows_total, bh), F32),
        scratch_shapes=[pltpu.VMEM((n, n, bh), F32), pltpu.VMEM((n, bh), F32)],
        compiler_params=_params(1),
        name="rwkv_scan",
    )(q)


def _outproj_kernel(x_ref, ylru_ref, yn_ref, bonus_ref, g_ref, lw_ref, lb_ref, wo1_ref, wo2_ref,
                    n2_ref, x1_ref, h2_ref, h8_ref, sh_ref):
    n = RWKV_HEAD_DIM
    per = LANES // n
    cols = []
    for cc in range(bonus_ref.shape[1] // LANES):
        blk = jnp.concatenate([yn_ref[(per * cc + hh) * N_PITCH:(per * cc + hh) * N_PITCH + n, :]
                               for hh in range(per)], axis=0)
        cols.append(blk.T)
    yn = jnp.concatenate(cols, axis=1)
    y_rwkv = ((yn * lw_ref[...] + lb_ref[...] + bonus_ref[...]) * g_ref[...]).astype(BF16)
    x1 = (x_ref[...]
          + jnp.dot(ylru_ref[...], wo1_ref[...], preferred_element_type=F32)
          + jnp.dot(y_rwkv, wo2_ref[...], preferred_element_type=F32))
    x1_ref[...] = x1
    ms = jnp.mean(x1 * x1, axis=-1, keepdims=True)
    h2 = x1 * lax.rsqrt(ms + NORM_EPS) * n2_ref[...]
    h2_ref[...] = h2.astype(BF16)
    sh = _fp8_scale(jnp.max(jnp.abs(h2), axis=-1, keepdims=True))
    sh_ref[...] = sh
    h8_ref[...] = (h2 / sh).astype(FP8)


def _outproj(x2, ylru, yn, bonus, g, lw, lb, wo1, wo2, n2, tm):
    t, d = x2.shape
    c = ylru.shape[1]
    s = yn.shape[1]
    nst = s // tm
    hrows = (c // RWKV_HEAD_DIM) * N_PITCH
    row_d = pl.BlockSpec((tm, d), lambda i: (i, 0))
    row_c = pl.BlockSpec((tm, c), lambda i: (i, 0))
    const = lambda i: (0, 0)
    return pl.pallas_call(
        _outproj_kernel,
        grid=(t // tm,),
        in_specs=[row_d, row_c, pl.BlockSpec((hrows, tm), lambda i: (i // nst, i % nst)), row_c, row_c,
                  pl.BlockSpec((1, c), const), pl.BlockSpec((1, c), const),
                  pl.BlockSpec((c, d), const), pl.BlockSpec((c, d), const),
                  pl.BlockSpec((1, d), const)],
        out_specs=[row_d, row_d, row_d, pl.BlockSpec((tm, 1), lambda i: (i, 0))],
        out_shape=[jax.ShapeDtypeStruct((t, d), F32), jax.ShapeDtypeStruct((t, d), BF16),
                   jax.ShapeDtypeStruct((t, d), FP8), jax.ShapeDtypeStruct((t, 1), F32)],
        compiler_params=_params(1),
        name="outproj",
    )(x2, ylru, yn, bonus, g, lw, lb, wo1, wo2, n2)


def _matmul_kernel(a_ref, w_ref, o_ref):
    o_ref[...] = jnp.dot(a_ref[...], w_ref[...], preferred_element_type=F32)


def _qproj(h2, wq, tm):
    t, d = h2.shape
    n = wq.shape[1]
    return pl.pallas_call(
        _matmul_kernel,
        grid=(t // tm,),
        in_specs=[pl.BlockSpec((tm, d), lambda i: (i, 0)),
                  pl.BlockSpec((d, n), lambda i: (0, 0))],
        out_specs=pl.BlockSpec((tm, n), lambda i: (i, 0)),
        out_shape=jax.ShapeDtypeStruct((t, n), F32),
        compiler_params=_params(1),
        name="qproj",
    )(h2, wq)


def _dot_nt3(a, b):
    dn = (((1,), (1,)), ((), ()))
    ah, al = _split_bf16(a)
    bh, bl = _split_bf16(b)
    return (lax.dot_general(ah, bh, dn, preferred_element_type=F32)
            + lax.dot_general(ah, bl, dn, preferred_element_type=F32)
            + lax.dot_general(al, bh, dn, preferred_element_type=F32))


def _tree(op, xs):
    xs = list(xs)
    while len(xs) > 1:
        nxt = [op(xs[i], xs[i + 1]) for i in range(0, len(xs) - 1, 2)]
        if len(xs) % 2:
            nxt.append(xs[-1])
        xs = nxt
    return xs[0]


def _route_kernel(q_ref, keys_ref, g_ref, ii_ref, jj_ref):
    tm = q_ref.shape[0]
    nk = PEER_NKEYS
    neg = -jnp.inf
    iota_n = lax.broadcasted_iota(jnp.int32, (nk, tm), 0).astype(F32)

    vals = [[None, None] for _ in range(PEER_HEADS)]
    idxs = [[None, None] for _ in range(PEER_HEADS)]
    for h in range(PEER_HEADS):
        for p in range(2):
            hp = 2 * h + p
            cur = _dot_nt3(keys_ref[hp], q_ref[:, hp * nk:(hp + 1) * nk])
            vs, ix = [], []
            for _ in range(PEER_TOPK):
                m = jnp.max(cur, axis=0, keepdims=True)
                idx = jnp.min(jnp.where(cur == m, iota_n, float(nk)), axis=0, keepdims=True)
                vs.append(m)
                ix.append(idx)
                cur = jnp.where(iota_n == idx, neg, cur)
            vals[h][p] = vs
            idxs[h][p] = ix

    def by_rank(table, p, rnk):
        return jnp.concatenate([table[h][p][rnk] for h in range(PEER_HEADS)], axis=0)

    va = [by_rank(vals, 0, r) for r in range(PEER_TOPK)]
    vb = [by_rank(vals, 1, r) for r in range(PEER_TOPK)]
    ia = [by_rank(idxs, 0, r) for r in range(PEER_TOPK)]
    ib = [by_rank(idxs, 1, r) for r in range(PEER_TOPK)]

    cand = []
    for p in range(PEER_TOPK):
        for qq in range(PEER_TOPK // (p + 1)):
            cand.append([va[p] + vb[qq], ia[p], ib[qq]])
    ncand = len(cand)

    best, sel_i, sel_j = [], [], []
    for _ in range(PEER_TOPK):
        m = _tree(jnp.maximum, [c[0] for c in cand])
        code = _tree(jnp.minimum, [jnp.where(c[0] == m, float(ci), float(ncand))
                                   for ci, c in enumerate(cand)])
        ci_sel = jnp.zeros_like(m)
        cj_sel = jnp.zeros_like(m)
        for ci, c in enumerate(cand):
            win = code == float(ci)
            ci_sel = jnp.where(win, c[1], ci_sel)
            cj_sel = jnp.where(win, c[2], cj_sel)
            c[0] = jnp.where(win, neg, c[0])
        best.append(m)
        sel_i.append(ci_sel)
        sel_j.append(cj_sel)

    ex = [jnp.exp(bv - best[0]) for bv in best]
    z = _tree(jnp.add, ex)
    gates = jnp.concatenate([e / z for e in ex], axis=0)
    g_ref[...] = gates.T
    ii_ref[...] = jnp.concatenate(sel_i, axis=0).T
    jj_ref[...] = jnp.concatenate(sel_j, axis=0).T


def _route(q, keys, tm):
    t, dq = q.shape
    nslot = PEER_TOPK * PEER_HEADS
    out = jax.ShapeDtypeStruct((t, nslot), F32)
    oblk = pl.BlockSpec((tm, nslot), lambda i: (i, 0))
    return pl.pallas_call(
        _route_kernel,
        grid=(t // tm,),
        in_specs=[pl.BlockSpec((tm, dq), lambda i: (i, 0)),
                  pl.BlockSpec(keys.shape, lambda i: (0, 0, 0))],
        out_specs=[oblk, oblk, oblk],
        out_shape=[out, out, out],
        compiler_params=_params(1),
        name="peer_route",
    )(q, keys)


def _wbuild_kernel(g_ref, ii_ref, jj_ref, sv_ref, w_ref, buf_ref):
    tw = g_ref.shape[0]
    nk = PEER_NKEYS
    iota_s = lax.broadcasted_iota(jnp.int32, (nk, g_ref.shape[1]), 0).astype(F32)
    dn = (((1,), (1,)), ((), ()))

    def tok(t, carry):
        g = g_ref[pl.ds(t, 1), :]
        a = jnp.where(iota_s == ii_ref[pl.ds(t, 1), :], g, 0.0).astype(BF16)
        bt = jnp.where(iota_s == jj_ref[pl.ds(t, 1), :], 1.0, 0.0).astype(BF16)
        off = pl.multiple_of(t * W_PITCH, SUBLANES)
        buf_ref[pl.ds(off, nk), :] = lax.dot_general(a, bt, dn, preferred_element_type=F32)
        return carry

    lax.fori_loop(0, tw, tok, 0, unroll=TOKEN_UNROLL)
    for i in range(nk):
        rows = buf_ref[pl.ds(i, tw, stride=W_PITCH), :]
        w_ref[:, i * nk:(i + 1) * nk] = (rows * sv_ref[:, i * nk:(i + 1) * nk]).astype(BF16)


def _wbuild(gates, ii, jj, sv, tw):
    t, nslot = gates.shape
    ne = PEER_NKEYS * PEER_NKEYS
    blk = pl.BlockSpec((tw, nslot), lambda i: (i, 0))
    return pl.pallas_call(
        _wbuild_kernel,
        grid=(t // tw,),
        in_specs=[blk, blk, blk, pl.BlockSpec((1, ne), lambda i: (0, 0))],
        out_specs=pl.BlockSpec((tw, ne), lambda i: (i, 0)),
        out_shape=jax.ShapeDtypeStruct((t, ne), BF16),
        scratch_shapes=[pltpu.VMEM((tw * W_PITCH, PEER_NKEYS), F32)],
        compiler_params=_params(1),
        name="peer_weights",
    )(gates, ii, jj, sv)


def _peer_step(h_ref, sh_ref, u_ref, su_ref, v_ref, w_ref, o_ref,
               act_new, act_old, p_new, s_new, p_old, s_old):
    tm = h_ref.shape[0]
    rc = tm // PEER_ROW_CHUNKS
    for c in range(PEER_ROW_CHUNKS):
        rows = slice(c * rc, (c + 1) * rc)
        a = (act_old[rows, :] * sh_ref[rows, :] * su_ref[...]).astype(BF16)
        p = jax.nn.gelu(a) * w_ref[rows, :]
        sp = _fp8_scale(jnp.max(jnp.abs(p), axis=-1, keepdims=True).astype(F32))
        p_new[rows, :] = (p.astype(F32) / sp).astype(FP8)
        s_new[rows, :] = sp
        act_new[rows, :] = jnp.dot(h_ref[rows, :], u_ref[...], preferred_element_type=F32)
        o_ref[rows, :] += jnp.dot(p_old[rows, :], v_ref[...], preferred_element_type=F32) * s_old[rows, :]


def _peer_kernel(h_ref, sh_ref, u_ref, su_ref, v_ref, w_ref, o_ref,
                 act0_ref, act1_ref, p0_ref, s0_ref, p1_ref, s1_ref):
    e = pl.program_id(1)

    @pl.when(e == 0)
    def _():
        o_ref[...] = jnp.zeros(o_ref.shape, F32)
        act1_ref[...] = jnp.zeros(act1_ref.shape, F32)
        p0_ref[...] = jnp.zeros(p0_ref.shape, FP8)
        s0_ref[...] = jnp.ones(s0_ref.shape, F32)

    args = (h_ref, sh_ref, u_ref, su_ref, v_ref, w_ref, o_ref)

    @pl.when(e % 2 == 0)
    def _():
        _peer_step(*args, act0_ref, act1_ref, p1_ref, s1_ref, p0_ref, s0_ref)

    @pl.when(e % 2 == 1)
    def _():
        _peer_step(*args, act1_ref, act0_ref, p0_ref, s0_ref, p1_ref, s1_ref)


def _peer(h8, sh, u8, su, v8, w, tm, te):
    t, d = h8.shape
    ne = u8.shape[1] // te
    clamp = lambda e: jnp.clip(e, 0, ne - 1)
    return pl.pallas_call(
        _peer_kernel,
        grid=(t // tm, ne + 2),
        in_specs=[pl.BlockSpec((tm, d), lambda i, e: (i, 0)),
                  pl.BlockSpec((tm, 1), lambda i, e: (i, 0)),
                  pl.BlockSpec((d, te), lambda i, e: (0, clamp(e))),
                  pl.BlockSpec((1, te), lambda i, e: (0, clamp(e - 1))),
                  pl.BlockSpec((te, d), lambda i, e: (clamp(e - 2), 0)),
                  pl.BlockSpec((tm, te), lambda i, e: (i, clamp(e - 1)))],
        out_specs=pl.BlockSpec((tm, d), lambda i, e: (i, 0)),
        out_shape=jax.ShapeDtypeStruct((t, d), F32),
        scratch_shapes=[pltpu.VMEM((tm, te), F32), pltpu.VMEM((tm, te), F32),
                        pltpu.VMEM((tm, te), FP8), pltpu.VMEM((tm, 1), F32),
                        pltpu.VMEM((tm, te), FP8), pltpu.VMEM((tm, 1), F32)],
        compiler_params=_params(2),
        name="peer_experts",
    )(h8, sh, u8, su, v8, w)


def _quantize_rows(tab):
    scale = _fp8_scale(jnp.max(jnp.abs(tab), axis=1, keepdims=True))
    return (tab / scale).astype(FP8), scale.reshape(1, -1)


def _final_kernel(x1_ref, p_ref, g_ref, o_ref):
    x = x1_ref[...] + p_ref[...]
    ms = jnp.mean(x * x, axis=-1, keepdims=True)
    o_ref[...] = x * lax.rsqrt(ms + NORM_EPS) * g_ref[...]


def _final(x1, peer, g, tm):
    t, d = x1.shape
    row = pl.BlockSpec((tm, d), lambda i: (i, 0))
    return pl.pallas_call(
        _final_kernel,
        grid=(t // tm,),
        in_specs=[row, row, pl.BlockSpec((1, d), lambda i: (0, 0))],
        out_specs=row,
        out_shape=jax.ShapeDtypeStruct((t, d), F32),
        compiler_params=_params(1),
        name="final_norm",
    )(x1, peer, g)


def _tiles(t, s):
    return dict(
        inproj_tm=min(1024, t), inproj_tn=512,
        seq_tt=min(256, s),
        scan_tt=min(16, s),
        outproj_tm=min(256, s),
        qproj_tm=min(512, t),
        route_tm=min(128, t),
        wbuild_tw=min(128, t),
        peer_tm=min(1024, t), peer_te=512,
        final_tm=min(512, t),
    )


def _layer(x2, b, s, norm1_g, w_in, conv_w, conv_b, lru_wa, lru_ba, lru_wi, lru_bi, lru_lambda,
           rwkv_mu, rwkv_w0, rwkv_w2, rwkv_a0, rwkv_a2, rwkv_g2, rwkv_k_k, rwkv_k_a, rwkv_r_k,
           rwkv_lnx_w, rwkv_lnx_b, w_out, norm2_g, peer_wq, peer_keys, peer_u, peer_v):
    t, d = x2.shape
    c_lru = conv_w.shape[1]
    c_rwkv = rwkv_w0.shape[0]
    n_heads = c_rwkv // RWKV_HEAD_DIM
    tl = _tiles(t, s)
    row = lambda v: v.reshape(1, -1)

    n_main = 2 * c_lru + 3 * c_rwkv
    n_lora = DECAY_LORA + AAA_LORA + GATE_LORA
    w_in_p = jnp.pad(w_in, ((0, 0), (0, LORA_PAD - n_lora))).astype(BF16)
    proj = _inproj(x2, row(norm1_g), w_in_p, tl["inproj_tm"], tl["inproj_tn"])

    y_lru = _lru(proj, b, s, tl["seq_tt"], conv_w, row(conv_b), lru_wa.astype(BF16), row(lru_ba),
                 lru_wi.astype(BF16), row(lru_bi), row(lru_lambda))

    mu_main, mu_lora = rwkv_mu[:3 * c_rwkv], rwkv_mu[3 * c_rwkv:]
    mu_l = jnp.pad(mu_lora, (0, LORA_PAD - n_lora))
    w2p = jnp.pad(rwkv_w2, ((0, LANES - DECAY_LORA), (0, 0))).astype(BF16)
    a2p = jnp.pad(rwkv_a2, ((DECAY_LORA, LANES - DECAY_LORA - AAA_LORA), (0, 0))).astype(BF16)
    g2p = jnp.pad(rwkv_g2, ((0, 2 * LANES - GATE_LORA), (0, 0))).astype(BF16)
    seg = np.arange(LANES) // RWKV_HEAD_DIM
    e_blk = jnp.asarray(seg[:, None] == seg[None, :], dtype=BF16)
    assert b * n_heads == LANES, "the scan maps (batch, head) onto the 128 lanes"
    scan_in, g_s, bonus = _rwkv_prep(
        proj, b, s, tl["seq_tt"], c_rwkv, 2 * c_lru,
        row(mu_main[:c_rwkv]), row(mu_main[c_rwkv:2 * c_rwkv]), row(mu_main[2 * c_rwkv:]), row(mu_l),
        row(rwkv_w0), w2p, row(rwkv_a0), a2p, g2p, row(rwkv_k_k), row(rwkv_k_a),
        row(rwkv_r_k), e_blk)

    yn = _to_head_major(_rwkv_scan(_to_time_major(scan_in), tl["scan_tt"]))

    w_out_bf = w_out.astype(BF16)
    x1, h2, h8, sh = _outproj(x2, y_lru, yn, bonus, g_s, row(rwkv_lnx_w), row(rwkv_lnx_b),
                              w_out_bf[:c_lru], w_out_bf[c_lru:], row(norm2_g), tl["outproj_tm"])

    q = _qproj(h2, peer_wq.astype(BF16), tl["qproj_tm"])
    keys = peer_keys.reshape(2 * PEER_HEADS, PEER_NKEYS, -1)
    gates, ii, jj = _route(q, keys, tl["route_tm"])
    u8, su = _quantize_rows(peer_u)
    u8 = u8.T
    v8, sv = _quantize_rows(peer_v)
    w = _wbuild(gates, ii, jj, sv, tl["wbuild_tw"])
    peer = _peer(h8, sh, u8, su, v8, w, tl["peer_tm"], tl["peer_te"])
    return x1, peer


def kernel(x, norm1_g, w_in, conv_w, conv_b, lru_wa, lru_ba, lru_wi, lru_bi, lru_lambda, rwkv_mu, rwkv_w0, rwkv_w2, rwkv_a0, rwkv_a2, rwkv_g2, rwkv_k_k, rwkv_k_a, rwkv_r_k, rwkv_lnx_w, rwkv_lnx_b, w_out, norm2_g, peer_wq, peer_keys, peer_u, peer_v, normf_g):
    b, s, d = x.shape
    depth = norm1_g.shape[0]
    x2 = x.reshape(b * s, d)
    peer = jnp.zeros_like(x2)
    for l in range(depth):
        if l > 0:
            x2 = x2 + peer
        x2, peer = _layer(x2, b, s, norm1_g[l], w_in[l], conv_w[l], conv_b[l], lru_wa[l], lru_ba[l],
                          lru_wi[l], lru_bi[l], lru_lambda[l], rwkv_mu[l], rwkv_w0[l], rwkv_w2[l],
                          rwkv_a0[l], rwkv_a2[l], rwkv_g2[l], rwkv_k_k[l], rwkv_k_a[l],
                          rwkv_r_k[l].reshape(-1), rwkv_lnx_w[l], rwkv_lnx_b[l], w_out[l], norm2_g[l],
                          peer_wq[l], peer_keys[l], peer_u[l], peer_v[l])
    out = _final(x2, peer, normf_g.reshape(1, -1), _tiles(b * s, s)["final_tm"])
    return out.reshape(b, s, d)
```

```python
import functools

import jax
import jax.numpy as jnp
import numpy as np
from jax import lax
from jax.experimental import pallas as pl
from jax.experimental.pallas import tpu as pltpu

NORM_EPS = 1e-6
GN_EPS = 64e-5
LRU_C = 8.0
LRU_HEADS = 8
CONV_WIDTH = 4
RWKV_HEAD_DIM = 64
DECAY_LORA = 64
AAA_LORA = 64
GATE_LORA = 160
PEER_HEADS = 8
PEER_NKEYS = 128
PEER_TOPK = 16

LANES = 128
SUBLANES = 8
LORA_PAD = 512
W_PITCH = 136
N_PITCH = 72
N_SCAN_IN = 6
PEER_ROW_CHUNKS = 4
TOKEN_UNROLL = 32
VMEM_LIMIT = 48 * 1024 * 1024
PEER_VMEM_LIMIT = 56 * 1024 * 1024

F32 = jnp.float32
BF16 = jnp.bfloat16
FP8 = jnp.float8_e4m3fn
FP8_TARGET = 256.0
TINY = 1e-30


def _fp8_scale(amax):
    return jnp.maximum(amax, TINY) * (1.0 / FP8_TARGET)


def _params(n_axes):
    return pltpu.CompilerParams(dimension_semantics=("arbitrary",) * n_axes,
                                vmem_limit_bytes=VMEM_LIMIT)


def _softplus(z):
    return jnp.maximum(z, 0.0) + jnp.log1p(jnp.exp(-jnp.abs(z)))


def _gelu_tanh_bf16(x):
    c = float(np.float32(np.sqrt(2.0 / np.pi)))
    k = float(np.float32(0.044715))
    arg = x * (c + (c * k) * (x * x))
    hx = x.astype(BF16) * 0.5
    return hx + hx * jnp.tanh(arg.astype(BF16))


def _split_bf16(x):
    hi = x.astype(BF16)
    lo = (x - hi.astype(F32)).astype(BF16)
    return hi, lo


def _segsum(x, e_blk):
    outs = []
    for c in range(x.shape[1] // LANES):
        hi, lo = _split_bf16(x[:, c * LANES:(c + 1) * LANES])
        outs.append(jnp.dot(hi, e_blk, preferred_element_type=F32)
                    + jnp.dot(lo, e_blk, preferred_element_type=F32))
    return jnp.concatenate(outs, axis=1)


def _inproj_kernel(x_ref, g_ref, w_ref, o_ref, h_ref):
    @pl.when(pl.program_id(1) == 0)
    def _():
        x = x_ref[...]
        ms = jnp.mean(x * x, axis=-1, keepdims=True)
        h_ref[...] = (x * lax.rsqrt(ms + NORM_EPS) * g_ref[...]).astype(BF16)

    o_ref[...] = jnp.dot(h_ref[...], w_ref[...], preferred_element_type=F32)


def _inproj(x2, g, w_bf, tm, tn):
    t, d = x2.shape
    n = w_bf.shape[1]
    return pl.pallas_call(
        _inproj_kernel,
        grid=(t // tm, n // tn),
        in_specs=[pl.BlockSpec((tm, d), lambda i, j: (i, 0)),
                  pl.BlockSpec((1, d), lambda i, j: (0, 0)),
                  pl.BlockSpec((d, tn), lambda i, j: (0, j))],
        out_specs=pl.BlockSpec((tm, tn), lambda i, j: (i, j)),
        out_shape=jax.ShapeDtypeStruct((t, n), F32),
        scratch_shapes=[pltpu.VMEM((tm, d), BF16)],
        compiler_params=_params(2),
        name="inproj",
    )(x2, g, w_bf)


def _lru_kernel(xb_ref, gate_ref, cw_ref, cb_ref, wa_ref, ba_ref, wi_ref, bi_ref, lam_ref,
                o_ref, ext_ref, a_ref, b_ref, hcar_ref):
    tt, c = xb_ref.shape
    hd = c // LRU_HEADS

    @pl.when(pl.program_id(1) == 0)
    def _():
        ext_ref[0:SUBLANES, :] = jnp.zeros((SUBLANES, c), F32)
        hcar_ref[...] = jnp.zeros((1, c), F32)

    x = xb_ref[...]
    ext_ref[SUBLANES:SUBLANES + tt, :] = x
    xc = cb_ref[...] + cw_ref[CONV_WIDTH - 1:CONV_WIDTH, :] * x
    for k in range(1, CONV_WIDTH):
        xc = xc + cw_ref[CONV_WIDTH - 1 - k:CONV_WIDTH - k, :] * ext_ref[SUBLANES - k:SUBLANES - k + tt, :]
    ext_ref[0:SUBLANES, :] = x[tt - SUBLANES:tt, :]

    xcb = xc.astype(BF16)
    ra, ia = [], []
    for h in range(LRU_HEADS):
        xh = xcb[:, h * hd:(h + 1) * hd]
        ra.append(jnp.dot(xh, wa_ref[h], preferred_element_type=F32))
        ia.append(jnp.dot(xh, wi_ref[h], preferred_element_type=F32))
    r = jax.nn.sigmoid(jnp.concatenate(ra, axis=1) + ba_ref[...])
    ig = jax.nn.sigmoid(jnp.concatenate(ia, axis=1) + bi_ref[...])
    log_a = (-LRU_C) * r * _softplus(-lam_ref[...])
    a = jnp.exp(log_a)
    bx = jnp.sqrt(-jnp.tanh(log_a) * (a * a + 1.0)) * (ig * xc)

    row = lax.broadcasted_iota(jnp.int32, (tt, c), 0) % SUBLANES
    for d in (1, 2, 4):
        m = row >= d
        a_sh = jnp.where(m, pltpu.roll(a, d, 0), 1.0)
        b_sh = jnp.where(m, pltpu.roll(bx, d, 0), 0.0)
        bx = a * b_sh + bx
        a = a * a_sh
    a_ref[...] = a
    b_ref[...] = bx

    def blk(i, hc):
        off = pl.multiple_of(i * SUBLANES, SUBLANES)
        hb = a_ref[pl.ds(off, SUBLANES), :] * hc + b_ref[pl.ds(off, SUBLANES), :]
        b_ref[pl.ds(off, SUBLANES), :] = hb
        return hb[SUBLANES - 1:SUBLANES, :]

    hcar_ref[...] = lax.fori_loop(0, tt // SUBLANES, blk, hcar_ref[...])
    o_ref[...] = (b_ref[...] * jax.nn.gelu(gate_ref[...])).astype(BF16)


def _lru(proj, b, s, tt, cw, cb, wa, ba, wi, bi, lam):
    c = cw.shape[1]
    nt = s // tt
    row = lambda bb, ss: (bb * nt + ss, 0)
    const2 = lambda bb, ss: (0, 0)
    const3 = lambda bb, ss: (0, 0, 0)
    hd = c // LRU_HEADS
    return pl.pallas_call(
        _lru_kernel,
        grid=(b, nt),
        in_specs=[pl.BlockSpec((tt, c), row),
                  pl.BlockSpec((tt, c), lambda bb, ss: (bb * nt + ss, 1)),
                  pl.BlockSpec((CONV_WIDTH, c), const2),
                  pl.BlockSpec((1, c), const2),
                  pl.BlockSpec((LRU_HEADS, hd, hd), const3),
                  pl.BlockSpec((1, c), const2),
                  pl.BlockSpec((LRU_HEADS, hd, hd), const3),
                  pl.BlockSpec((1, c), const2),
                  pl.BlockSpec((1, c), const2)],
        out_specs=pl.BlockSpec((tt, c), row),
        out_shape=jax.ShapeDtypeStruct((b * s, c), BF16),
        scratch_shapes=[pltpu.VMEM((tt + SUBLANES, c), F32),
                        pltpu.VMEM((tt, c), F32),
                        pltpu.VMEM((tt, c), F32),
                        pltpu.VMEM((1, c), F32)],
        compiler_params=_params(2),
        name="rg_lru",
    )(proj, proj, cw, cb, wa, ba, wi, bi, lam)


def _token_shift(ext_ref, x, mu):
    tt = x.shape[0]
    ext_ref[SUBLANES:SUBLANES + tt, :] = x
    prev = ext_ref[SUBLANES - 1:SUBLANES - 1 + tt, :]
    ext_ref[0:SUBLANES, :] = x[tt - SUBLANES:tt, :]
    return x + (prev - x) * mu


def _store_head_major(o_ref, q, x):
    n = RWKV_HEAD_DIM
    per = LANES // n
    for cc in range(x.shape[1] // LANES):
        xt = x[:, cc * LANES:(cc + 1) * LANES].T
        for hh in range(per):
            h = per * cc + hh
            o_ref[q, h * N_PITCH:h * N_PITCH + n, :] = xt[hh * n:(hh + 1) * n, :]
            o_ref[q, h * N_PITCH + n:(h + 1) * N_PITCH, :] = jnp.zeros((N_PITCH - n, x.shape[0]), F32)


def _rwkv_prep_kernel(r_ref, k_ref, v_ref, l_ref, mur_ref, muk_ref, muv_ref, mul_ref,
                      w0_ref, w2_ref, a0_ref, a2_ref, g2_ref, kk_ref, ka_ref, rk_ref, e_ref,
                      so_ref, go_ref, bonus_ref,
                      er_ref, ek_ref, ev_ref, el_ref):
    @pl.when(pl.program_id(1) == 0)
    def _():
        for ext in (er_ref, ek_ref, ev_ref, el_ref):
            ext[0:SUBLANES, :] = jnp.zeros((SUBLANES, ext.shape[1]), F32)

    r = _token_shift(er_ref, r_ref[...], mur_ref[...])
    k = _token_shift(ek_ref, k_ref[...], muk_ref[...])
    v = _token_shift(ev_ref, v_ref[...], muv_ref[...])
    lo = _token_shift(el_ref, l_ref[...], mul_ref[...])

    lo_a = lo[:, 0:LANES]
    lo_g = lo[:, LANES:3 * LANES]
    wl = w0_ref[...] + jnp.dot(jnp.tanh(lo_a).astype(BF16), w2_ref[...], preferred_element_type=F32)
    w_log = -_softplus(-wl) - 0.5
    decay = jnp.exp(-jnp.exp(w_log))
    a = jax.nn.sigmoid(a0_ref[...] + jnp.dot(lo_a.astype(BF16), a2_ref[...], preferred_element_type=F32))
    g = jnp.dot(jax.nn.sigmoid(lo_g).astype(BF16), g2_ref[...], preferred_element_type=F32)

    e_blk = e_ref[...]
    kk = k * kk_ref[...]
    kk = kk / jnp.maximum(jnp.sqrt(_segsum(kk * kk, e_blk)), 1e-12)
    kmod = k * (1.0 + (a - 1.0) * ka_ref[...])

    for q, val in enumerate((r, decay, kmod, v, -kk, kk * a)):
        _store_head_major(so_ref, q, val)
    go_ref[...] = g
    bonus_ref[...] = _segsum(r * kmod * rk_ref[...], e_blk) * v


def _rwkv_prep(proj, b, s, tt, c, col0, mur, muk, muv, mul, w0, w2p, a0, a2p, g2p, kk, ka, rk, e_blk):
    nt = s // tt
    cb0 = col0 // c
    lb = (col0 + 3 * c) // LORA_PAD
    const2 = lambda bb, ss: (0, 0)
    vec = pl.BlockSpec((1, c), const2)
    row = lambda bb, ss: (bb * nt + ss, 0)
    out = jax.ShapeDtypeStruct((b * s, c), F32)
    hrows = (c // RWKV_HEAD_DIM) * N_PITCH
    return pl.pallas_call(
        _rwkv_prep_kernel,
        grid=(b, nt),
        in_specs=[pl.BlockSpec((tt, c), lambda bb, ss: (bb * nt + ss, cb0)),
                  pl.BlockSpec((tt, c), lambda bb, ss: (bb * nt + ss, cb0 + 1)),
                  pl.BlockSpec((tt, c), lambda bb, ss: (bb * nt + ss, cb0 + 2)),
                  pl.BlockSpec((tt, LORA_PAD), lambda bb, ss: (bb * nt + ss, lb)),
                  vec, vec, vec, pl.BlockSpec((1, LORA_PAD), const2),
                  vec, pl.BlockSpec((LANES, c), const2),
                  vec, pl.BlockSpec((LANES, c), const2),
                  pl.BlockSpec((2 * LANES, c), const2),
                  vec, vec, vec, pl.BlockSpec((LANES, LANES), const2)],
        out_specs=[pl.BlockSpec((N_SCAN_IN, hrows, tt), lambda bb, ss: (0, bb, ss)),
                   pl.BlockSpec((tt, c), row), pl.BlockSpec((tt, c), row)],
        out_shape=[jax.ShapeDtypeStruct((N_SCAN_IN, b * hrows, s), F32), out, out],
        scratch_shapes=[pltpu.VMEM((tt + SUBLANES, c), F32),
                        pltpu.VMEM((tt + SUBLANES, c), F32),
                        pltpu.VMEM((tt + SUBLANES, c), F32),
                        pltpu.VMEM((tt + SUBLANES, LORA_PAD), F32)],
        compiler_params=_params(2),
        name="rwkv_prep",
    )(proj, proj, proj, proj, mur, muk, muv, mul, w0, w2p, a0, a2p, g2p, kk, ka, rk, e_blk)


def _swap_kernel(i_ref, o_ref):
    for n in range(RWKV_HEAD_DIM):
        o_ref[pl.ds(n, LANES, stride=N_PITCH), :] = i_ref[pl.ds(n, LANES, stride=N_PITCH), :].T
    for n in range(RWKV_HEAD_DIM, N_PITCH):
        o_ref[pl.ds(n, LANES, stride=N_PITCH), :] = jnp.zeros((LANES, LANES), F32)


def _to_time_major(p):
    nq, rows, s = p.shape
    return pl.pallas_call(
        _swap_kernel,
        grid=(nq, s // LANES),
        in_specs=[pl.BlockSpec((None, rows, LANES), lambda q, i: (q, 0, i))],
        out_specs=pl.BlockSpec((None, rows, LANES), lambda q, i: (q, i, 0)),
        out_shape=jax.ShapeDtypeStruct((nq, s * N_PITCH, LANES), F32),
        compiler_params=_params(2),
        name="to_time_major",
    )(p)


def _to_head_major(y):
    rows_total, bh = y.shape
    s = rows_total // N_PITCH
    rows = LANES * N_PITCH
    return pl.pallas_call(
        _swap_kernel,
        grid=(s // LANES,),
        in_specs=[pl.BlockSpec((rows, LANES), lambda i: (i, 0))],
        out_specs=pl.BlockSpec((rows, LANES), lambda i: (0, i)),
        out_shape=jax.ShapeDtypeStruct((bh * N_PITCH, s), F32),
        compiler_params=_params(1),
        name="to_head_major",
    )(y)


def _rwkv_scan_kernel(q_ref, y_ref, s_ref, sa_ref):
    n = RWKV_HEAD_DIM
    tt = y_ref.shape[0] // N_PITCH

    @pl.when(pl.program_id(0) == 0)
    def _():
        s_ref[...] = jnp.zeros(s_ref.shape, F32)

    def step(t, carry):
        base = pl.multiple_of(t * N_PITCH, SUBLANES)
        r = q_ref[0, pl.ds(base, n), :]
        w = q_ref[1, pl.ds(base, n), :]
        k = q_ref[2, pl.ds(base, n), :]
        na = q_ref[4, pl.ds(base, n), :]
        b = q_ref[5, pl.ds(base, n), :]
        for i in range(n):
            sa_ref[i:i + 1, :] = jnp.sum(s_ref[i] * na, axis=0, keepdims=True)
        for i in range(n):
            s_new = s_ref[i] * w + sa_ref[i:i + 1, :] * b + q_ref[3, pl.ds(base + i, 1), :] * k
            s_ref[i] = s_new
            y_ref[pl.ds(base + i, 1), :] = jnp.sum(s_new * r, axis=0, keepdims=True)
        y = y_ref[pl.ds(base, n), :]
        mean = jnp.mean(y, axis=0, keepdims=True)
        yc = y - mean
        var = jnp.mean(yc * yc, axis=0, keepdims=True)
        y_ref[pl.ds(base, n), :] = yc * lax.rsqrt(var + GN_EPS)
        y_ref[pl.ds(base + n, N_PITCH - n), :] = jnp.zeros((N_PITCH - n, y_ref.shape[1]), F32)
        return carry

    lax.fori_loop(0, tt, step, 0)


def _rwkv_scan(q, tt):
    nq, rows_total, bh = q.shape
    s = rows_total // N_PITCH
    n = RWKV_HEAD_DIM
    return pl.pallas_call(
        _rwkv_scan_kernel,
        grid=(s // tt,),
        in_specs=[pl.BlockSpec((nq, tt * N_PITCH, bh), lambda i: (0, i, 0))],
        out_specs=pl.BlockSpec((tt * N_PITCH, bh), lambda i: (i, 0)),
        out_shape=jax.ShapeDtypeStruct((rows_total, bh), F32),
        scratch_shapes=[pltpu.VMEM((n, n, bh), F32), pltpu.VMEM((n, bh), F32)],
        compiler_params=_params(1),
        name="rwkv_scan",
    )(q)


def _outproj_kernel(x_ref, ylru_ref, yn_ref, bonus_ref, g_ref, lw_ref, lb_ref, wo1_ref, wo2_ref,
                    n2_ref, x1_ref, h2_ref, h8_ref, sh_ref):
    n = RWKV_HEAD_DIM
    per = LANES // n
    cols = []
    for cc in range(bonus_ref.shape[1] // LANES):
        blk = jnp.concatenate([yn_ref[(per * cc + hh) * N_PITCH:(per * cc + hh) * N_PITCH + n, :]
                               for hh in range(per)], axis=0)
        cols.append(blk.T)
    yn = jnp.concatenate(cols, axis=1)
    y_rwkv = ((yn * lw_ref[...] + lb_ref[...] + bonus_ref[...]) * g_ref[...]).astype(BF16)
    x1 = (x_ref[...]
          + jnp.dot(ylru_ref[...], wo1_ref[...], preferred_element_type=F32)
          + jnp.dot(y_rwkv, wo2_ref[...], preferred_element_type=F32))
    x1_ref[...] = x1
    ms = jnp.mean(x1 * x1, axis=-1, keepdims=True)
    h2 = x1 * lax.rsqrt(ms + NORM_EPS) * n2_ref[...]
    h2_ref[...] = h2.astype(BF16)
    sh = _fp8_scale(jnp.max(jnp.abs(h2), axis=-1, keepdims=True))
    sh_ref[...] = sh
    h8_ref[...] = (h2 / sh).astype(FP8)


def _outproj(x2, ylru, yn, bonus, g, lw, lb, wo1, wo2, n2, tm):
    t, d = x2.shape
    c = ylru.shape[1]
    s = yn.shape[1]
    nst = s // tm
    hrows = (c // RWKV_HEAD_DIM) * N_PITCH
    row_d = pl.BlockSpec((tm, d), lambda i: (i, 0))
    row_c = pl.BlockSpec((tm, c), lambda i: (i, 0))
    const = lambda i: (0, 0)
    return pl.pallas_call(
        _outproj_kernel,
        grid=(t // tm,),
        in_specs=[row_d, row_c, pl.BlockSpec((hrows, tm), lambda i: (i // nst, i % nst)), row_c, row_c,
                  pl.BlockSpec((1, c), const), pl.BlockSpec((1, c), const),
                  pl.BlockSpec((c, d), const), pl.BlockSpec((c, d), const),
                  pl.BlockSpec((1, d), const)],
        out_specs=[row_d, row_d, row_d, pl.BlockSpec((tm, 1), lambda i: (i, 0))],
        out_shape=[jax.ShapeDtypeStruct((t, d), F32), jax.ShapeDtypeStruct((t, d), BF16),
                   jax.ShapeDtypeStruct((t, d), FP8), jax.ShapeDtypeStruct((t, 1), F32)],
        compiler_params=_params(1),
        name="outproj",
    )(x2, ylru, yn, bonus, g, lw, lb, wo1, wo2, n2)


def _matmul_kernel(a_ref, w_ref, o_ref):
    o_ref[...] = jnp.dot(a_ref[...], w_ref[...], preferred_element_type=F32)


def _qproj(h2, wq, tm):
    t, d = h2.shape
    n = wq.shape[1]
    return pl.pallas_call(
        _matmul_kernel,
        grid=(t // tm,),
        in_specs=[pl.BlockSpec((tm, d), lambda i: (i, 0)),
                  pl.BlockSpec((d, n), lambda i: (0, 0))],
        out_specs=pl.BlockSpec((tm, n), lambda i: (i, 0)),
        out_shape=jax.ShapeDtypeStruct((t, n), F32),
        compiler_params=_params(1),
        name="qproj",
    )(h2, wq)


def _dot_nt3(a, b):
    dn = (((1,), (1,)), ((), ()))
    ah, al = _split_bf16(a)
    bh, bl = _split_bf16(b)
    return (lax.dot_general(ah, bh, dn, preferred_element_type=F32)
            + lax.dot_general(ah, bl, dn, preferred_element_type=F32)
            + lax.dot_general(al, bh, dn, preferred_element_type=F32))


def _tree(op, xs):
    xs = list(xs)
    while len(xs) > 1:
        nxt = [op(xs[i], xs[i + 1]) for i in range(0, len(xs) - 1, 2)]
        if len(xs) % 2:
            nxt.append(xs[-1])
        xs = nxt
    return xs[0]


def _route_kernel(q_ref, keys_ref, g_ref, ii_ref, jj_ref):
    tm = q_ref.shape[0]
    nk = PEER_NKEYS
    neg = -jnp.inf
    iota_n = lax.broadcasted_iota(jnp.int32, (nk, tm), 0).astype(F32)

    vals = [[None, None] for _ in range(PEER_HEADS)]
    idxs = [[None, None] for _ in range(PEER_HEADS)]
    for h in range(PEER_HEADS):
        for p in range(2):
            hp = 2 * h + p
            cur = _dot_nt3(keys_ref[hp], q_ref[:, hp * nk:(hp + 1) * nk])
            vs, ix = [], []
            for _ in range(PEER_TOPK):
                m = jnp.max(cur, axis=0, keepdims=True)
                idx = jnp.min(jnp.where(cur == m, iota_n, float(nk)), axis=0, keepdims=True)
                vs.append(m)
                ix.append(idx)
                cur = jnp.where(iota_n == idx, neg, cur)
            vals[h][p] = vs
            idxs[h][p] = ix

    def by_rank(table, p, rnk):
        return jnp.concatenate([table[h][p][rnk] for h in range(PEER_HEADS)], axis=0)

    va = [by_rank(vals, 0, r) for r in range(PEER_TOPK)]
    vb = [by_rank(vals, 1, r) for r in range(PEER_TOPK)]
    ia = [by_rank(idxs, 0, r) for r in range(PEER_TOPK)]
    ib = [by_rank(idxs, 1, r) for r in range(PEER_TOPK)]

    cand = []
    for p in range(PEER_TOPK):
        for qq in range(PEER_TOPK // (p + 1)):
            cand.append([va[p] + vb[qq], ia[p], ib[qq]])
    ncand = len(cand)

    best, sel_i, sel_j = [], [], []
    for _ in range(PEER_TOPK):
        m = _tree(jnp.maximum, [c[0] for c in cand])
        code = _tree(jnp.minimum, [jnp.where(c[0] == m, float(ci), float(ncand))
                                   for ci, c in enumerate(cand)])
        ci_sel = jnp.zeros_like(m)
        cj_sel = jnp.zeros_like(m)
        for ci, c in enumerate(cand):
            win = code == float(ci)
            ci_sel = jnp.where(win, c[1], ci_sel)
            cj_sel = jnp.where(win, c[2], cj_sel)
            c[0] = jnp.where(win, neg, c[0])
        best.append(m)
        sel_i.append(ci_sel)
        sel_j.append(cj_sel)

    ex = [jnp.exp(bv - best[0]) for bv in best]
    z = _tree(jnp.add, ex)
    gates = jnp.concatenate([e / z for e in ex], axis=0)
    g_ref[...] = gates.T
    ii_ref[...] = jnp.concatenate(sel_i, axis=0).T
    jj_ref[...] = jnp.concatenate(sel_j, axis=0).T


def _route(q, keys, tm):
    t, dq = q.shape
    nslot = PEER_TOPK * PEER_HEADS
    out = jax.ShapeDtypeStruct((t, nslot), F32)
    oblk = pl.BlockSpec((tm, nslot), lambda i: (i, 0))
    return pl.pallas_call(
        _route_kernel,
        grid=(t // tm,),
        in_specs=[pl.BlockSpec((tm, dq), lambda i: (i, 0)),
                  pl.BlockSpec(keys.shape, lambda i: (0, 0, 0))],
        out_specs=[oblk, oblk, oblk],
        out_shape=[out, out, out],
        compiler_params=_params(1),
        name="peer_route",
    )(q, keys)


def _wbuild_kernel(g_ref, ii_ref, jj_ref, sv_ref, w_ref, buf_ref):
    tw = g_ref.shape[0]
    nk = PEER_NKEYS
    iota_s = lax.broadcasted_iota(jnp.int32, (nk, g_ref.shape[1]), 0).astype(F32)
    dn = (((1,), (1,)), ((), ()))

    def token_block(t):
        g = g_ref[pl.ds(t, 1), :]
        a = jnp.where(iota_s == ii_ref[pl.ds(t, 1), :], g, 0.0).astype(BF16)
        bt = jnp.where(iota_s == jj_ref[pl.ds(t, 1), :], 1.0, 0.0).astype(BF16)
        return lax.dot_general(a, bt, dn, preferred_element_type=F32) * sv_ref[...]

    def tok(t, carry):
        off = pl.multiple_of(t * W_PITCH, SUBLANES)
        buf_ref[pl.ds(off, nk), :] = token_block(t)
        return carry

    lax.fori_loop(0, tw, tok, 0, unroll=TOKEN_UNROLL)
    for i in range(nk):
        w_ref[:, i * nk:(i + 1) * nk] = buf_ref[pl.ds(i, tw, stride=W_PITCH), :].astype(BF16)


def _wbuild(gates, ii, jj, sv, tw):
    t, nslot = gates.shape
    ne = PEER_NKEYS * PEER_NKEYS
    blk = pl.BlockSpec((tw, nslot), lambda i: (i, 0))
    return pl.pallas_call(
        _wbuild_kernel,
        grid=(t // tw,),
        in_specs=[blk, blk, blk, pl.BlockSpec((PEER_NKEYS, PEER_NKEYS), lambda i: (0, 0))],
        out_specs=pl.BlockSpec((tw, ne), lambda i: (i, 0)),
        out_shape=jax.ShapeDtypeStruct((t, ne), BF16),
        scratch_shapes=[pltpu.VMEM((tw * W_PITCH, PEER_NKEYS), F32)],
        compiler_params=_params(1),
        name="peer_weights",
    )(gates, ii, jj, sv.reshape(PEER_NKEYS, PEER_NKEYS))


def _peer_step(h_ref, sh_ref, u_ref, su_ref, v_ref, w_ref, o_ref,
               act_new, act_old, p_new, s_new, p_old, s_old):
    tm = h_ref.shape[0]
    rc = tm // PEER_ROW_CHUNKS
    for c in range(PEER_ROW_CHUNKS):
        rows = slice(c * rc, (c + 1) * rc)
        a = act_old[rows, :] * sh_ref[rows, :] * su_ref[...]
        p = _gelu_tanh_bf16(a) * w_ref[rows, :]
        sp = _fp8_scale(jnp.max(jnp.abs(p), axis=-1, keepdims=True).astype(F32))
        p_new[rows, :] = (p.astype(F32) / sp).astype(FP8)
        s_new[rows, :] = sp
        act_new[rows, :] = lax.dot_general(h_ref[rows, :], u_ref[...], (((1,), (1,)), ((), ())),
                                           preferred_element_type=F32)
        o_ref[rows, :] += jnp.dot(p_old[rows, :], v_ref[...], preferred_element_type=F32) * s_old[rows, :]


def _peer_kernel(h_ref, sh_ref, u_ref, su_ref, v_ref, w_ref, x1_ref, gf_ref, o_ref,
                 act0_ref, act1_ref, p0_ref, s0_ref, p1_ref, s1_ref, *, final_norm):
    e = pl.program_id(1)

    @pl.when(e == 0)
    def _():
        o_ref[...] = jnp.zeros(o_ref.shape, F32)
        act1_ref[...] = jnp.zeros(act1_ref.shape, F32)
        p0_ref[...] = jnp.zeros(p0_ref.shape, FP8)
        s0_ref[...] = jnp.ones(s0_ref.shape, F32)

    args = (h_ref, sh_ref, u_ref, su_ref, v_ref, w_ref, o_ref)

    @pl.when(e % 2 == 0)
    def _():
        _peer_step(*args, act0_ref, act1_ref, p1_ref, s1_ref, p0_ref, s0_ref)

    @pl.when(e % 2 == 1)
    def _():
        _peer_step(*args, act1_ref, act0_ref, p0_ref, s0_ref, p1_ref, s1_ref)

    @pl.when(e == pl.num_programs(1) - 1)
    def _():
        x = x1_ref[...] + o_ref[...]
        if final_norm:
            ms = jnp.mean(x * x, axis=-1, keepdims=True)
            x = x * lax.rsqrt(ms + NORM_EPS) * gf_ref[...]
        o_ref[...] = x


def _peer(h8, sh, u8, su, v8, w, x1, gf, final_norm, tm, te):
    t, d = h8.shape
    ne = u8.shape[0] // te
    clamp = lambda e: jnp.clip(e, 0, ne - 1)
    return pl.pallas_call(
        functools.partial(_peer_kernel, final_norm=final_norm),
        grid=(t // tm, ne + 2),
        in_specs=[pl.BlockSpec((tm, d), lambda i, e: (i, 0)),
                  pl.BlockSpec((tm, 1), lambda i, e: (i, 0)),
                  pl.BlockSpec((te, d), lambda i, e: (clamp(e), 0)),
                  pl.BlockSpec((1, te), lambda i, e: (0, clamp(e - 1))),
                  pl.BlockSpec((te, d), lambda i, e: (clamp(e - 2), 0)),
                  pl.BlockSpec((tm, te), lambda i, e: (i, clamp(e - 1))),
                  pl.BlockSpec((tm, d), lambda i, e: (i, 0), pipeline_mode=pl.Buffered(1)),
                  pl.BlockSpec((1, d), lambda i, e: (0, 0))],
        out_specs=pl.BlockSpec((tm, d), lambda i, e: (i, 0)),
        out_shape=jax.ShapeDtypeStruct((t, d), F32),
        scratch_shapes=[pltpu.VMEM((tm, te), F32), pltpu.VMEM((tm, te), F32),
                        pltpu.VMEM((tm, te), FP8), pltpu.VMEM((tm, 1), F32),
                        pltpu.VMEM((tm, te), FP8), pltpu.VMEM((tm, 1), F32)],
        compiler_params=pltpu.CompilerParams(dimension_semantics=("arbitrary", "arbitrary"),
                                             vmem_limit_bytes=PEER_VMEM_LIMIT),
        name="peer_experts",
    )(h8, sh, u8, su, v8, w, x1, gf)


def _quantize_kernel(t_ref, q_ref, s_ref):
    x = t_ref[...]
    scale = _fp8_scale(jnp.max(jnp.abs(x), axis=1, keepdims=True))
    s_ref[...] = scale
    q_ref[...] = (x / scale).astype(FP8)


def _quantize_rows(tab, tr):
    n, d = tab.shape
    q, scale = pl.pallas_call(
        _quantize_kernel,
        grid=(n // tr,),
        in_specs=[pl.BlockSpec((tr, d), lambda i: (i, 0))],
        out_specs=[pl.BlockSpec((tr, d), lambda i: (i, 0)), pl.BlockSpec((tr, 1), lambda i: (i, 0))],
        out_shape=[jax.ShapeDtypeStruct((n, d), FP8), jax.ShapeDtypeStruct((n, 1), F32)],
        compiler_params=_params(1),
        name="quantize_rows",
    )(tab)
    return q, scale.reshape(1, n)


def _tiles(t, s):
    return dict(
        inproj_tm=min(1024, t), inproj_tn=512,
        seq_tt=min(256, s),
        scan_tt=min(16, s),
        outproj_tm=min(256, s),
        qproj_tm=min(512, t),
        route_tm=min(128, t),
        wbuild_tw=min(128, t),
        peer_tm=min(1024, t), peer_te=512,
        quant_tr=512,
    )


def _layer(x2, b, s, final_g, norm1_g, w_in, conv_w, conv_b, lru_wa, lru_ba, lru_wi, lru_bi, lru_lambda,
           rwkv_mu, rwkv_w0, rwkv_w2, rwkv_a0, rwkv_a2, rwkv_g2, rwkv_k_k, rwkv_k_a, rwkv_r_k,
           rwkv_lnx_w, rwkv_lnx_b, w_out, norm2_g, peer_wq, peer_keys, peer_u, peer_v):
    t, d = x2.shape
    c_lru = conv_w.shape[1]
    c_rwkv = rwkv_w0.shape[0]
    n_heads = c_rwkv // RWKV_HEAD_DIM
    tl = _tiles(t, s)
    row = lambda v: v.reshape(1, -1)

    n_main = 2 * c_lru + 3 * c_rwkv
    n_lora = DECAY_LORA + AAA_LORA + GATE_LORA
    w_in_p = jnp.pad(w_in, ((0, 0), (0, LORA_PAD - n_lora))).astype(BF16)
    proj = _inproj(x2, row(norm1_g), w_in_p, tl["inproj_tm"], tl["inproj_tn"])

    y_lru = _lru(proj, b, s, tl["seq_tt"], conv_w, row(conv_b), lru_wa.astype(BF16), row(lru_ba),
                 lru_wi.astype(BF16), row(lru_bi), row(lru_lambda))

    mu_main, mu_lora = rwkv_mu[:3 * c_rwkv], rwkv_mu[3 * c_rwkv:]
    mu_l = jnp.pad(mu_lora, (0, LORA_PAD - n_lora))
    w2p = jnp.pad(rwkv_w2, ((0, LANES - DECAY_LORA), (0, 0))).astype(BF16)
    a2p = jnp.pad(rwkv_a2, ((DECAY_LORA, LANES - DECAY_LORA - AAA_LORA), (0, 0))).astype(BF16)
    g2p = jnp.pad(rwkv_g2, ((0, 2 * LANES - GATE_LORA), (0, 0))).astype(BF16)
    seg = np.arange(LANES) // RWKV_HEAD_DIM
    e_blk = jnp.asarray(seg[:, None] == seg[None, :], dtype=BF16)
    assert b * n_heads == LANES, "the scan maps (batch, head) onto the 128 lanes"
    scan_in, g_s, bonus = _rwkv_prep(
        proj, b, s, tl["seq_tt"], c_rwkv, 2 * c_lru,
        row(mu_main[:c_rwkv]), row(mu_main[c_rwkv:2 * c_rwkv]), row(mu_main[2 * c_rwkv:]), row(mu_l),
        row(rwkv_w0), w2p, row(rwkv_a0), a2p, g2p, row(rwkv_k_k), row(rwkv_k_a),
        row(rwkv_r_k), e_blk)

    yn = _to_head_major(_rwkv_scan(_to_time_major(scan_in), tl["scan_tt"]))

    w_out_bf = w_out.astype(BF16)
    x1, h2, h8, sh = _outproj(x2, y_lru, yn, bonus, g_s, row(rwkv_lnx_w), row(rwkv_lnx_b),
                              w_out_bf[:c_lru], w_out_bf[c_lru:], row(norm2_g), tl["outproj_tm"])

    q = _qproj(h2, peer_wq.astype(BF16), tl["qproj_tm"])
    keys = peer_keys.reshape(2 * PEER_HEADS, PEER_NKEYS, -1)
    gates, ii, jj = _route(q, keys, tl["route_tm"])
    u8, su = _quantize_rows(peer_u, tl["quant_tr"])
    v8, sv = _quantize_rows(peer_v, tl["quant_tr"])
    w = _wbuild(gates, ii, jj, sv, tl["wbuild_tw"])
    gf = jnp.ones((1, d), F32) if final_g is None else final_g.reshape(1, d)
    return _peer(h8, sh, u8, su, v8, w, x1, gf, final_g is not None, tl["peer_tm"], tl["peer_te"])


def kernel(x, norm1_g, w_in, conv_w, conv_b, lru_wa, lru_ba, lru_wi, lru_bi, lru_lambda, rwkv_mu, rwkv_w0, rwkv_w2, rwkv_a0, rwkv_a2, rwkv_g2, rwkv_k_k, rwkv_k_a, rwkv_r_k, rwkv_lnx_w, rwkv_lnx_b, w_out, norm2_g, peer_wq, peer_keys, peer_u, peer_v, normf_g):
    b, s, d = x.shape
    depth = norm1_g.shape[0]
    x2 = x.reshape(b * s, d)
    for l in range(depth):
        final_g = normf_g if l == depth - 1 else None
        x2 = _layer(x2, b, s, final_g, norm1_g[l], w_in[l], conv_w[l], conv_b[l], lru_wa[l], lru_ba[l],
                    lru_wi[l], lru_bi[l], lru_lambda[l], rwkv_mu[l], rwkv_w0[l], rwkv_w2[l],
                    rwkv_a0[l], rwkv_a2[l], rwkv_g2[l], rwkv_k_k[l], rwkv_k_a[l],
                    rwkv_r_k[l].reshape(-1), rwkv_lnx_w[l], rwkv_lnx_b[l], w_out[l], norm2_g[l],
                    peer_wq[l], peer_keys[l], peer_u[l], peer_v[l])
    return x2.reshape(b, s, d)
```

```python
import functools

import jax
import jax.numpy as jnp
import numpy as np
from jax import lax
from jax.experimental import pallas as pl
from jax.experimental.pallas import tpu as pltpu

NORM_EPS = 1e-6
GN_EPS = 64e-5
LRU_C = 8.0
LRU_HEADS = 8
CONV_WIDTH = 4
RWKV_HEAD_DIM = 64
DECAY_LORA = 64
AAA_LORA = 64
GATE_LORA = 160
PEER_HEADS = 8
PEER_NKEYS = 128
PEER_TOPK = 16

LANES = 128
SUBLANES = 8
LORA_PAD = 512
W_PITCH = 136
N_PITCH = 72
N_SCAN_IN = 6
PEER_ROW_CHUNKS = 4
TOKEN_UNROLL = 32
VMEM_LIMIT = 48 * 1024 * 1024
PEER_VMEM_LIMIT = 56 * 1024 * 1024

F32 = jnp.float32
BF16 = jnp.bfloat16
FP8 = jnp.float8_e4m3fn
FP8_TARGET = 256.0
TINY = 1e-30


def _fp8_scale(amax):
    return jnp.maximum(amax, TINY) * (1.0 / FP8_TARGET)


def _params(n_axes):
    return pltpu.CompilerParams(dimension_semantics=("arbitrary",) * n_axes,
                                vmem_limit_bytes=VMEM_LIMIT)


def _softplus(z):
    return jnp.maximum(z, 0.0) + jnp.log1p(jnp.exp(-jnp.abs(z)))


def _gelu_tanh_bf16(x):
    c = float(np.float32(np.sqrt(2.0 / np.pi)))
    k = float(np.float32(0.044715))
    arg = x * (c + (c * k) * (x * x))
    hx = x.astype(BF16) * 0.5
    return hx + hx * jnp.tanh(arg.astype(BF16))


def _split_bf16(x):
    hi = x.astype(BF16)
    lo = (x - hi.astype(F32)).astype(BF16)
    return hi, lo


def _segsum(x, e_blk):
    outs = []
    for c in range(x.shape[1] // LANES):
        hi, lo = _split_bf16(x[:, c * LANES:(c + 1) * LANES])
        outs.append(jnp.dot(hi, e_blk, preferred_element_type=F32)
                    + jnp.dot(lo, e_blk, preferred_element_type=F32))
    return jnp.concatenate(outs, axis=1)


def _inproj_kernel(x_ref, g_ref, w_ref, o_ref, h_ref):
    @pl.when(pl.program_id(1) == 0)
    def _():
        x = x_ref[...]
        ms = jnp.mean(x * x, axis=-1, keepdims=True)
        h_ref[...] = (x * lax.rsqrt(ms + NORM_EPS) * g_ref[...]).astype(BF16)

    o_ref[...] = jnp.dot(h_ref[...], w_ref[...], preferred_element_type=F32)


def _inproj(x2, g, w_bf, tm, tn):
    t, d = x2.shape
    n = w_bf.shape[1]
    return pl.pallas_call(
        _inproj_kernel,
        grid=(t // tm, n // tn),
        in_specs=[pl.BlockSpec((tm, d), lambda i, j: (i, 0)),
                  pl.BlockSpec((1, d), lambda i, j: (0, 0)),
                  pl.BlockSpec((d, tn), lambda i, j: (0, j))],
        out_specs=pl.BlockSpec((tm, tn), lambda i, j: (i, j)),
        out_shape=jax.ShapeDtypeStruct((t, n), F32),
        scratch_shapes=[pltpu.VMEM((tm, d), BF16)],
        compiler_params=_params(2),
        name="inproj",
    )(x2, g, w_bf)


def _lru_kernel(xb_ref, gate_ref, cw_ref, cb_ref, wa_ref, ba_ref, wi_ref, bi_ref, lam_ref,
                o_ref, ext_ref, a_ref, b_ref, hcar_ref):
    tt, c = xb_ref.shape
    hd = c // LRU_HEADS

    @pl.when(pl.program_id(1) == 0)
    def _():
        ext_ref[0:SUBLANES, :] = jnp.zeros((SUBLANES, c), F32)
        hcar_ref[...] = jnp.zeros((1, c), F32)

    x = xb_ref[...]
    ext_ref[SUBLANES:SUBLANES + tt, :] = x
    xc = cb_ref[...] + cw_ref[CONV_WIDTH - 1:CONV_WIDTH, :] * x
    for k in range(1, CONV_WIDTH):
        xc = xc + cw_ref[CONV_WIDTH - 1 - k:CONV_WIDTH - k, :] * ext_ref[SUBLANES - k:SUBLANES - k + tt, :]
    ext_ref[0:SUBLANES, :] = x[tt - SUBLANES:tt, :]

    xcb = xc.astype(BF16)
    ra, ia = [], []
    for h in range(LRU_HEADS):
        xh = xcb[:, h * hd:(h + 1) * hd]
        ra.append(jnp.dot(xh, wa_ref[h], preferred_element_type=F32))
        ia.append(jnp.dot(xh, wi_ref[h], preferred_element_type=F32))
    r = jax.nn.sigmoid(jnp.concatenate(ra, axis=1) + ba_ref[...])
    ig = jax.nn.sigmoid(jnp.concatenate(ia, axis=1) + bi_ref[...])
    log_a = (-LRU_C) * r * _softplus(-lam_ref[...])
    a = jnp.exp(log_a)
    bx = jnp.sqrt(-jnp.tanh(log_a) * (a * a + 1.0)) * (ig * xc)

    row = lax.broadcasted_iota(jnp.int32, (tt, c), 0) % SUBLANES
    for d in (1, 2, 4):
        m = row >= d
        a_sh = jnp.where(m, pltpu.roll(a, d, 0), 1.0)
        b_sh = jnp.where(m, pltpu.roll(bx, d, 0), 0.0)
        bx = a * b_sh + bx
        a = a * a_sh
    a_ref[...] = a
    b_ref[...] = bx

    def blk(i, hc):
        off = pl.multiple_of(i * SUBLANES, SUBLANES)
        hb = a_ref[pl.ds(off, SUBLANES), :] * hc + b_ref[pl.ds(off, SUBLANES), :]
        b_ref[pl.ds(off, SUBLANES), :] = hb
        return hb[SUBLANES - 1:SUBLANES, :]

    hcar_ref[...] = lax.fori_loop(0, tt // SUBLANES, blk, hcar_ref[...])
    o_ref[...] = (b_ref[...] * jax.nn.gelu(gate_ref[...])).astype(BF16)


def _lru(proj, b, s, tt, cw, cb, wa, ba, wi, bi, lam):
    c = cw.shape[1]
    nt = s // tt
    row = lambda bb, ss: (bb * nt + ss, 0)
    const2 = lambda bb, ss: (0, 0)
    const3 = lambda bb, ss: (0, 0, 0)
    hd = c // LRU_HEADS
    return pl.pallas_call(
        _lru_kernel,
        grid=(b, nt),
        in_specs=[pl.BlockSpec((tt, c), row),
                  pl.BlockSpec((tt, c), lambda bb, ss: (bb * nt + ss, 1)),
                  pl.BlockSpec((CONV_WIDTH, c), const2),
                  pl.BlockSpec((1, c), const2),
                  pl.BlockSpec((LRU_HEADS, hd, hd), const3),
                  pl.BlockSpec((1, c), const2),
                  pl.BlockSpec((LRU_HEADS, hd, hd), const3),
                  pl.BlockSpec((1, c), const2),
                  pl.BlockSpec((1, c), const2)],
        out_specs=pl.BlockSpec((tt, c), row),
        out_shape=jax.ShapeDtypeStruct((b * s, c), BF16),
        scratch_shapes=[pltpu.VMEM((tt + SUBLANES, c), F32),
                        pltpu.VMEM((tt, c), F32),
                        pltpu.VMEM((tt, c), F32),
                        pltpu.VMEM((1, c), F32)],
        compiler_params=_params(2),
        name="rg_lru",
    )(proj, proj, cw, cb, wa, ba, wi, bi, lam)


def _token_shift(ext_ref, x, mu):
    tt = x.shape[0]
    ext_ref[SUBLANES:SUBLANES + tt, :] = x
    prev = ext_ref[SUBLANES - 1:SUBLANES - 1 + tt, :]
    ext_ref[0:SUBLANES, :] = x[tt - SUBLANES:tt, :]
    return x + (prev - x) * mu


def _store_head_major(o_ref, q, x):
    n = RWKV_HEAD_DIM
    per = LANES // n
    for cc in range(x.shape[1] // LANES):
        xt = x[:, cc * LANES:(cc + 1) * LANES].T
        for hh in range(per):
            h = per * cc + hh
            o_ref[q, h * N_PITCH:h * N_PITCH + n, :] = xt[hh * n:(hh + 1) * n, :]
            o_ref[q, h * N_PITCH + n:(h + 1) * N_PITCH, :] = jnp.zeros((N_PITCH - n, x.shape[0]), F32)


def _rwkv_prep_kernel(r_ref, k_ref, v_ref, l_ref, mur_ref, muk_ref, muv_ref, mul_ref,
                      w0_ref, w2_ref, a0_ref, a2_ref, g2_ref, kk_ref, ka_ref, rk_ref, e_ref,
                      so_ref, go_ref, bonus_ref,
                      er_ref, ek_ref, ev_ref, el_ref):
    @pl.when(pl.program_id(1) == 0)
    def _():
        for ext in (er_ref, ek_ref, ev_ref, el_ref):
            ext[0:SUBLANES, :] = jnp.zeros((SUBLANES, ext.shape[1]), F32)

    r = _token_shift(er_ref, r_ref[...], mur_ref[...])
    k = _token_shift(ek_ref, k_ref[...], muk_ref[...])
    v = _token_shift(ev_ref, v_ref[...], muv_ref[...])
    lo = _token_shift(el_ref, l_ref[...], mul_ref[...])

    lo_a = lo[:, 0:LANES]
    lo_g = lo[:, LANES:3 * LANES]
    wl = w0_ref[...] + jnp.dot(jnp.tanh(lo_a).astype(BF16), w2_ref[...], preferred_element_type=F32)
    w_log = -_softplus(-wl) - 0.5
    decay = jnp.exp(-jnp.exp(w_log))
    a = jax.nn.sigmoid(a0_ref[...] + jnp.dot(lo_a.astype(BF16), a2_ref[...], preferred_element_type=F32))
    g = jnp.dot(jax.nn.sigmoid(lo_g).astype(BF16), g2_ref[...], preferred_element_type=F32)

    e_blk = e_ref[...]
    kk = k * kk_ref[...]
    kk = kk / jnp.maximum(jnp.sqrt(_segsum(kk * kk, e_blk)), 1e-12)
    kmod = k * (1.0 + (a - 1.0) * ka_ref[...])

    for q, val in enumerate((r, decay, kmod, v, -kk, kk * a)):
        _store_head_major(so_ref, q, val)
    go_ref[...] = g
    bonus_ref[...] = _segsum(r * kmod * rk_ref[...], e_blk) * v


def _rwkv_prep(proj, b, s, tt, c, col0, mur, muk, muv, mul, w0, w2p, a0, a2p, g2p, kk, ka, rk, e_blk):
    nt = s // tt
    cb0 = col0 // c
    lb = (col0 + 3 * c) // LORA_PAD
    const2 = lambda bb, ss: (0, 0)
    vec = pl.BlockSpec((1, c), const2)
    row = lambda bb, ss: (bb * nt + ss, 0)
    out = jax.ShapeDtypeStruct((b * s, c), F32)
    hrows = (c // RWKV_HEAD_DIM) * N_PITCH
    return pl.pallas_call(
        _rwkv_prep_kernel,
        grid=(b, nt),
        in_specs=[pl.BlockSpec((tt, c), lambda bb, ss: (bb * nt + ss, cb0)),
                  pl.BlockSpec((tt, c), lambda bb, ss: (bb * nt + ss, cb0 + 1)),
                  pl.BlockSpec((tt, c), lambda bb, ss: (bb * nt + ss, cb0 + 2)),
                  pl.BlockSpec((tt, LORA_PAD), lambda bb, ss: (bb * nt + ss, lb)),
                  vec, vec, vec, pl.BlockSpec((1, LORA_PAD), const2),
                  vec, pl.BlockSpec((LANES, c), const2),
                  vec, pl.BlockSpec((LANES, c), const2),
                  pl.BlockSpec((2 * LANES, c), const2),
                  vec, vec, vec, pl.BlockSpec((LANES, LANES), const2)],
        out_specs=[pl.BlockSpec((N_SCAN_IN, hrows, tt), lambda bb, ss: (0, bb, ss)),
                   pl.BlockSpec((tt, c), row), pl.BlockSpec((tt, c), row)],
        out_shape=[jax.ShapeDtypeStruct((N_SCAN_IN, b * hrows, s), F32), out, out],
        scratch_shapes=[pltpu.VMEM((tt + SUBLANES, c), F32),
                        pltpu.VMEM((tt + SUBLANES, c), F32),
                        pltpu.VMEM((tt + SUBLANES, c), F32),
                        pltpu.VMEM((tt + SUBLANES, LORA_PAD), F32)],
        compiler_params=_params(2),
        name="rwkv_prep",
    )(proj, proj, proj, proj, mur, muk, muv, mul, w0, w2p, a0, a2p, g2p, kk, ka, rk, e_blk)


def _swap_kernel(i_ref, o_ref):
    for n in range(RWKV_HEAD_DIM):
        o_ref[pl.ds(n, LANES, stride=N_PITCH), :] = i_ref[pl.ds(n, LANES, stride=N_PITCH), :].T
    for n in range(RWKV_HEAD_DIM, N_PITCH):
        o_ref[pl.ds(n, LANES, stride=N_PITCH), :] = jnp.zeros((LANES, LANES), F32)


def _to_time_major(p):
    nq, rows, s = p.shape
    return pl.pallas_call(
        _swap_kernel,
        grid=(nq, s // LANES),
        in_specs=[pl.BlockSpec((None, rows, LANES), lambda q, i: (q, 0, i))],
        out_specs=pl.BlockSpec((None, rows, LANES), lambda q, i: (q, i, 0)),
        out_shape=jax.ShapeDtypeStruct((nq, s * N_PITCH, LANES), F32),
        compiler_params=_params(2),
        name="to_time_major",
    )(p)


def _to_head_major(y):
    rows_total, bh = y.shape
    s = rows_total // N_PITCH
    rows = LANES * N_PITCH
    return pl.pallas_call(
        _swap_kernel,
        grid=(s // LANES,),
        in_specs=[pl.BlockSpec((rows, LANES), lambda i: (i, 0))],
        out_specs=pl.BlockSpec((rows, LANES), lambda i: (0, i)),
        out_shape=jax.ShapeDtypeStruct((bh * N_PITCH, s), F32),
        compiler_params=_params(1),
        name="to_head_major",
    )(y)


def _rwkv_scan_kernel(q_ref, y_ref, s_ref, sa_ref):
    n = RWKV_HEAD_DIM
    tt = y_ref.shape[0] // N_PITCH

    @pl.when(pl.program_id(0) == 0)
    def _():
        s_ref[...] = jnp.zeros(s_ref.shape, F32)

    def step(t, carry):
        base = pl.multiple_of(t * N_PITCH, SUBLANES)
        r = q_ref[0, pl.ds(base, n), :]
        w = q_ref[1, pl.ds(base, n), :]
        k = q_ref[2, pl.ds(base, n), :]
        na = q_ref[4, pl.ds(base, n), :]
        b = q_ref[5, pl.ds(base, n), :]
        for i in range(n):
            sa_ref[i:i + 1, :] = jnp.sum(s_ref[i] * na, axis=0, keepdims=True)
        for i in range(n):
            s_new = s_ref[i] * w + sa_ref[i:i + 1, :] * b + q_ref[3, pl.ds(base + i, 1), :] * k
            s_ref[i] = s_new
            y_ref[pl.ds(base + i, 1), :] = jnp.sum(s_new * r, axis=0, keepdims=True)
        y = y_ref[pl.ds(base, n), :]
        mean = jnp.mean(y, axis=0, keepdims=True)
        yc = y - mean
        var = jnp.mean(yc * yc, axis=0, keepdims=True)
        y_ref[pl.ds(base, n), :] = yc * lax.rsqrt(var + GN_EPS)
        y_ref[pl.ds(base + n, N_PITCH - n), :] = jnp.zeros((N_PITCH - n, y_ref.shape[1]), F32)
        return carry

    lax.fori_loop(0, tt, step, 0)


def _rwkv_scan(q, tt):
    nq, rows_total, bh = q.shape
    s = rows_total // N_PITCH
    n = RWKV_HEAD_DIM
    return pl.pallas_call(
        _rwkv_scan_kernel,
        grid=(s // tt,),
        in_specs=[pl.BlockSpec((nq, tt * N_PITCH, bh), lambda i: (0, i, 0))],
        out_specs=pl.BlockSpec((tt * N_PITCH, bh), lambda i: (i, 0)),
        out_shape=jax.ShapeDtypeStruct((rows_total, bh), F32),
        scratch_shapes=[pltpu.VMEM((n, n, bh), F32), pltpu.VMEM((n, bh), F32)],
        compiler_params=_params(1),
        name="rwkv_scan",
    )(q)


def _outproj_kernel(x_ref, ylru_ref, yn_ref, bonus_ref, g_ref, lw_ref, lb_ref, wo1_ref, wo2_ref,
                    n2_ref, x1_ref, h2_ref, h8_ref, sh_ref):
    n = RWKV_HEAD_DIM
    per = LANES // n
    cols = []
    for cc in range(bonus_ref.shape[1] // LANES):
        blk = jnp.concatenate([yn_ref[(per * cc + hh) * N_PITCH:(per * cc + hh) * N_PITCH + n, :]
                               for hh in range(per)], axis=0)
        cols.append(blk.T)
    yn = jnp.concatenate(cols, axis=1)
    y_rwkv = ((yn * lw_ref[...] + lb_ref[...] + bonus_ref[...]) * g_ref[...]).astype(BF16)
    x1 = (x_ref[...]
          + jnp.dot(ylru_ref[...], wo1_ref[...], preferred_element_type=F32)
          + jnp.dot(y_rwkv, wo2_ref[...], preferred_element_type=F32))
    x1_ref[...] = x1
    ms = jnp.mean(x1 * x1, axis=-1, keepdims=True)
    h2 = x1 * lax.rsqrt(ms + NORM_EPS) * n2_ref[...]
    h2_ref[...] = h2.astype(BF16)
    sh = _fp8_scale(jnp.max(jnp.abs(h2), axis=-1, keepdims=True))
    sh_ref[...] = sh
    h8_ref[...] = (h2 / sh).astype(FP8)


def _outproj(x2, ylru, yn, bonus, g, lw, lb, wo1, wo2, n2, tm):
    t, d = x2.shape
    c = ylru.shape[1]
    s = yn.shape[1]
    nst = s // tm
    hrows = (c // RWKV_HEAD_DIM) * N_PITCH
    row_d = pl.BlockSpec((tm, d), lambda i: (i, 0))
    row_c = pl.BlockSpec((tm, c), lambda i: (i, 0))
    const = lambda i: (0, 0)
    return pl.pallas_call(
        _outproj_kernel,
        grid=(t // tm,),
        in_specs=[row_d, row_c, pl.BlockSpec((hrows, tm), lambda i: (i // nst, i % nst)), row_c, row_c,
                  pl.BlockSpec((1, c), const), pl.BlockSpec((1, c), const),
                  pl.BlockSpec((c, d), const), pl.BlockSpec((c, d), const),
                  pl.BlockSpec((1, d), const)],
        out_specs=[row_d, row_d, row_d, pl.BlockSpec((tm, 1), lambda i: (i, 0))],
        out_shape=[jax.ShapeDtypeStruct((t, d), F32), jax.ShapeDtypeStruct((t, d), BF16),
                   jax.ShapeDtypeStruct((t, d), FP8), jax.ShapeDtypeStruct((t, 1), F32)],
        compiler_params=_params(1),
        name="outproj",
    )(x2, ylru, yn, bonus, g, lw, lb, wo1, wo2, n2)


def _matmul_kernel(a_ref, w_ref, o_ref):
    o_ref[...] = jnp.dot(a_ref[...], w_ref[...], preferred_element_type=F32)


def _qproj(h2, wq, tm):
    t, d = h2.shape
    n = wq.shape[1]
    return pl.pallas_call(
        _matmul_kernel,
        grid=(t // tm,),
        in_specs=[pl.BlockSpec((tm, d), lambda i: (i, 0)),
                  pl.BlockSpec((d, n), lambda i: (0, 0))],
        out_specs=pl.BlockSpec((tm, n), lambda i: (i, 0)),
        out_shape=jax.ShapeDtypeStruct((t, n), F32),
        compiler_params=_params(1),
        name="qproj",
    )(h2, wq)


def _dot_nt3(a, b):
    dn = (((1,), (1,)), ((), ()))
    ah, al = _split_bf16(a)
    bh, bl = _split_bf16(b)
    return (lax.dot_general(ah, bh, dn, preferred_element_type=F32)
            + lax.dot_general(ah, bl, dn, preferred_element_type=F32)
            + lax.dot_general(al, bh, dn, preferred_element_type=F32))


def _tree(op, xs):
    xs = list(xs)
    while len(xs) > 1:
        nxt = [op(xs[i], xs[i + 1]) for i in range(0, len(xs) - 1, 2)]
        if len(xs) % 2:
            nxt.append(xs[-1])
        xs = nxt
    return xs[0]


def _route_kernel(q_ref, keys_ref, g_ref, ii_ref, jj_ref):
    tm = q_ref.shape[0]
    nk = PEER_NKEYS
    neg = -jnp.inf
    iota_n = lax.broadcasted_iota(jnp.int32, (nk, tm), 0).astype(F32)

    vals = [[None, None] for _ in range(PEER_HEADS)]
    idxs = [[None, None] for _ in range(PEER_HEADS)]
    for h in range(PEER_HEADS):
        for p in range(2):
            hp = 2 * h + p
            cur = _dot_nt3(keys_ref[hp], q_ref[:, hp * nk:(hp + 1) * nk])
            vs, ix = [], []
            for _ in range(PEER_TOPK):
                m = jnp.max(cur, axis=0, keepdims=True)
                idx = jnp.min(jnp.where(cur == m, iota_n, float(nk)), axis=0, keepdims=True)
                vs.append(m)
                ix.append(idx)
                cur = jnp.where(iota_n == idx, neg, cur)
            vals[h][p] = vs
            idxs[h][p] = ix

    def by_rank(table, p, rnk):
        return jnp.concatenate([table[h][p][rnk] for h in range(PEER_HEADS)], axis=0)

    va = [by_rank(vals, 0, r) for r in range(PEER_TOPK)]
    vb = [by_rank(vals, 1, r) for r in range(PEER_TOPK)]
    ia = [by_rank(idxs, 0, r) for r in range(PEER_TOPK)]
    ib = [by_rank(idxs, 1, r) for r in range(PEER_TOPK)]

    cand = []
    for p in range(PEER_TOPK):
        for qq in range(PEER_TOPK // (p + 1)):
            cand.append([va[p] + vb[qq], ia[p], ib[qq]])
    ncand = len(cand)

    best, sel_i, sel_j = [], [], []
    for _ in range(PEER_TOPK):
        m = _tree(jnp.maximum, [c[0] for c in cand])
        code = _tree(jnp.minimum, [jnp.where(c[0] == m, float(ci), float(ncand))
                                   for ci, c in enumerate(cand)])
        ci_sel = jnp.zeros_like(m)
        cj_sel = jnp.zeros_like(m)
        for ci, c in enumerate(cand):
            win = code == float(ci)
            ci_sel = jnp.where(win, c[1], ci_sel)
            cj_sel = jnp.where(win, c[2], cj_sel)
            c[0] = jnp.where(win, neg, c[0])
        best.append(m)
        sel_i.append(ci_sel)
        sel_j.append(cj_sel)

    ex = [jnp.exp(bv - best[0]) for bv in best]
    z = _tree(jnp.add, ex)
    gates = jnp.concatenate([e / z for e in ex], axis=0)
    g_ref[...] = gates.T
    ii_ref[...] = jnp.concatenate(sel_i, axis=0).T
    jj_ref[...] = jnp.concatenate(sel_j, axis=0).T


def _route(q, keys, tm):
    t, dq = q.shape
    nslot = PEER_TOPK * PEER_HEADS
    out = jax.ShapeDtypeStruct((t, nslot), F32)
    oblk = pl.BlockSpec((tm, nslot), lambda i: (i, 0))
    return pl.pallas_call(
        _route_kernel,
        grid=(t // tm,),
        in_specs=[pl.BlockSpec((tm, dq), lambda i: (i, 0)),
                  pl.BlockSpec(keys.shape, lambda i: (0, 0, 0))],
        out_specs=[oblk, oblk, oblk],
        out_shape=[out, out, out],
        compiler_params=_params(1),
        name="peer_route",
    )(q, keys)


def _wbuild_kernel(g_ref, ii_ref, jj_ref, sv_ref, w_ref, buf_ref):
    tw = g_ref.shape[0]
    nk = PEER_NKEYS
    iota_s = lax.broadcasted_iota(jnp.int32, (nk, g_ref.shape[1]), 0).astype(F32)
    dn = (((1,), (1,)), ((), ()))

    def token_block(t):
        g = g_ref[pl.ds(t, 1), :]
        a = jnp.where(iota_s == ii_ref[pl.ds(t, 1), :], g, 0.0).astype(BF16)
        bt = jnp.where(iota_s == jj_ref[pl.ds(t, 1), :], 1.0, 0.0).astype(BF16)
        return lax.dot_general(a, bt, dn, preferred_element_type=F32) * sv_ref[...]

    def tok(t, carry):
        off = pl.multiple_of(t * W_PITCH, SUBLANES)
        buf_ref[pl.ds(off, nk), :] = token_block(t)
        return carry

    lax.fori_loop(0, tw, tok, 0, unroll=TOKEN_UNROLL)
    for i in range(nk):
        w_ref[:, i * nk:(i + 1) * nk] = buf_ref[pl.ds(i, tw, stride=W_PITCH), :].astype(BF16)


def _wbuild(gates, ii, jj, sv, tw):
    t, nslot = gates.shape
    ne = PEER_NKEYS * PEER_NKEYS
    blk = pl.BlockSpec((tw, nslot), lambda i: (i, 0))
    return pl.pallas_call(
        _wbuild_kernel,
        grid=(t // tw,),
        in_specs=[blk, blk, blk, pl.BlockSpec((PEER_NKEYS, PEER_NKEYS), lambda i: (0, 0))],
        out_specs=pl.BlockSpec((tw, ne), lambda i: (i, 0)),
        out_shape=jax.ShapeDtypeStruct((t, ne), BF16),
        scratch_shapes=[pltpu.VMEM((tw * W_PITCH, PEER_NKEYS), F32)],
        compiler_params=_params(1),
        name="peer_weights",
    )(gates, ii, jj, sv.reshape(PEER_NKEYS, PEER_NKEYS))


def _peer_step(h_ref, sh_ref, u_ref, su_ref, v_ref, w_ref, o_ref,
               act_new, act_old, p_new, s_new, p_old, s_old):
    tm = h_ref.shape[0]
    rc = tm // PEER_ROW_CHUNKS
    for c in range(PEER_ROW_CHUNKS):
        rows = slice(c * rc, (c + 1) * rc)
        a = act_old[rows, :] * sh_ref[rows, :] * su_ref[...]
        p = _gelu_tanh_bf16(a) * w_ref[rows, :]
        sp = _fp8_scale(jnp.max(jnp.abs(p), axis=-1, keepdims=True).astype(F32))
        p_new[rows, :] = (p.astype(F32) / sp).astype(FP8)
        s_new[rows, :] = sp
        act_new[rows, :] = lax.dot_general(h_ref[rows, :], u_ref[...], (((1,), (1,)), ((), ())),
                                           preferred_element_type=F32)
        o_ref[rows, :] += jnp.dot(p_old[rows, :], v_ref[...], preferred_element_type=F32) * s_old[rows, :]


def _peer_kernel(h_ref, sh_ref, u_ref, su_ref, v_ref, w_ref, x1_ref, gf_ref, o_ref,
                 act0_ref, act1_ref, p0_ref, s0_ref, p1_ref, s1_ref, *, ne, final_norm):
    g = pl.program_id(0)
    e_out = (g - 2) % ne

    @pl.when(g == 0)
    def _():
        act1_ref[...] = jnp.zeros(act1_ref.shape, F32)
        p0_ref[...] = jnp.zeros(p0_ref.shape, FP8)
        s0_ref[...] = jnp.ones(s0_ref.shape, F32)

    @pl.when((g < 2) | (e_out == 0))
    def _():
        o_ref[...] = jnp.zeros(o_ref.shape, F32)

    args = (h_ref, sh_ref, u_ref, su_ref, v_ref, w_ref, o_ref)

    @pl.when(g % 2 == 0)
    def _():
        _peer_step(*args, act0_ref, act1_ref, p1_ref, s1_ref, p0_ref, s0_ref)

    @pl.when(g % 2 == 1)
    def _():
        _peer_step(*args, act1_ref, act0_ref, p0_ref, s0_ref, p1_ref, s1_ref)

    @pl.when((g >= 2) & (e_out == ne - 1))
    def _():
        x = x1_ref[...] + o_ref[...]
        if final_norm:
            ms = jnp.mean(x * x, axis=-1, keepdims=True)
            x = x * lax.rsqrt(ms + NORM_EPS) * gf_ref[...]
        o_ref[...] = x


def _peer(h8, sh, u8, su, v8, w, x1, gf, final_norm, tm, te):
    t, d = h8.shape
    ne = u8.shape[0] // te
    nt = t // tm
    tok = lambda g: jnp.clip(g // ne, 0, nt - 1)
    return pl.pallas_call(
        functools.partial(_peer_kernel, ne=ne, final_norm=final_norm),
        grid=(nt * ne + 2,),
        in_specs=[pl.BlockSpec((tm, d), lambda g: (tok(g), 0)),
                  pl.BlockSpec((tm, 1), lambda g: (tok(g - 1), 0)),
                  pl.BlockSpec((te, d), lambda g: (g % ne, 0)),
                  pl.BlockSpec((1, te), lambda g: (0, (g - 1) % ne)),
                  pl.BlockSpec((te, d), lambda g: ((g - 2) % ne, 0)),
                  pl.BlockSpec((tm, te), lambda g: (tok(g - 1), (g - 1) % ne)),
                  pl.BlockSpec((tm, d), lambda g: (tok(g - 2), 0), pipeline_mode=pl.Buffered(1)),
                  pl.BlockSpec((1, d), lambda g: (0, 0))],
        out_specs=pl.BlockSpec((tm, d), lambda g: (tok(g - 2), 0)),
        out_shape=jax.ShapeDtypeStruct((t, d), F32),
        scratch_shapes=[pltpu.VMEM((tm, te), F32), pltpu.VMEM((tm, te), F32),
                        pltpu.VMEM((tm, te), FP8), pltpu.VMEM((tm, 1), F32),
                        pltpu.VMEM((tm, te), FP8), pltpu.VMEM((tm, 1), F32)],
        compiler_params=pltpu.CompilerParams(dimension_semantics=("arbitrary",),
                                             vmem_limit_bytes=PEER_VMEM_LIMIT),
        name="peer_experts",
    )(h8, sh, u8, su, v8, w, x1, gf)


def _quantize_kernel(t_ref, q_ref, s_ref):
    x = t_ref[...]
    scale = _fp8_scale(jnp.max(jnp.abs(x), axis=1, keepdims=True))
    s_ref[...] = scale
    q_ref[...] = (x / scale).astype(FP8)


def _quantize_rows(tab, tr):
    n, d = tab.shape
    q, scale = pl.pallas_call(
        _quantize_kernel,
        grid=(n // tr,),
        in_specs=[pl.BlockSpec((tr, d), lambda i: (i, 0))],
        out_specs=[pl.BlockSpec((tr, d), lambda i: (i, 0)), pl.BlockSpec((tr, 1), lambda i: (i, 0))],
        out_shape=[jax.ShapeDtypeStruct((n, d), FP8), jax.ShapeDtypeStruct((n, 1), F32)],
        compiler_params=_params(1),
        name="quantize_rows",
    )(tab)
    return q, scale.reshape(1, n)


def _tiles(t, s):
    return dict(
        inproj_tm=min(1024, t), inproj_tn=512,
        seq_tt=min(256, s),
        scan_tt=min(32, s),
        outproj_tm=min(256, s),
        qproj_tm=min(512, t),
        route_tm=min(128, t),
        wbuild_tw=min(128, t),
        peer_tm=min(1024, t), peer_te=512,
        quant_tr=512,
    )


def _layer(x2, b, s, final_g, norm1_g, w_in, conv_w, conv_b, lru_wa, lru_ba, lru_wi, lru_bi, lru_lambda,
           rwkv_mu, rwkv_w0, rwkv_w2, rwkv_a0, rwkv_a2, rwkv_g2, rwkv_k_k, rwkv_k_a, rwkv_r_k,
           rwkv_lnx_w, rwkv_lnx_b, w_out, norm2_g, peer_wq, peer_keys, peer_u, peer_v):
    t, d = x2.shape
    c_lru = conv_w.shape[1]
    c_rwkv = rwkv_w0.shape[0]
    n_heads = c_rwkv // RWKV_HEAD_DIM
    tl = _tiles(t, s)
    row = lambda v: v.reshape(1, -1)

    n_main = 2 * c_lru + 3 * c_rwkv
    n_lora = DECAY_LORA + AAA_LORA + GATE_LORA
    w_in_p = jnp.pad(w_in, ((0, 0), (0, LORA_PAD - n_lora))).astype(BF16)
    proj = _inproj(x2, row(norm1_g), w_in_p, tl["inproj_tm"], tl["inproj_tn"])

    y_lru = _lru(proj, b, s, tl["seq_tt"], conv_w, row(conv_b), lru_wa.astype(BF16), row(lru_ba),
                 lru_wi.astype(BF16), row(lru_bi), row(lru_lambda))

    mu_main, mu_lora = rwkv_mu[:3 * c_rwkv], rwkv_mu[3 * c_rwkv:]
    mu_l = jnp.pad(mu_lora, (0, LORA_PAD - n_lora))
    w2p = jnp.pad(rwkv_w2, ((0, LANES - DECAY_LORA), (0, 0))).astype(BF16)
    a2p = jnp.pad(rwkv_a2, ((DECAY_LORA, LANES - DECAY_LORA - AAA_LORA), (0, 0))).astype(BF16)
    g2p = jnp.pad(rwkv_g2, ((0, 2 * LANES - GATE_LORA), (0, 0))).astype(BF16)
    seg = np.arange(LANES) // RWKV_HEAD_DIM
    e_blk = jnp.asarray(seg[:, None] == seg[None, :], dtype=BF16)
    assert b * n_heads == LANES, "the scan maps (batch, head) onto the 128 lanes"
    scan_in, g_s, bonus = _rwkv_prep(
        proj, b, s, tl["seq_tt"], c_rwkv, 2 * c_lru,
        row(mu_main[:c_rwkv]), row(mu_main[c_rwkv:2 * c_rwkv]), row(mu_main[2 * c_rwkv:]), row(mu_l),
        row(rwkv_w0), w2p, row(rwkv_a0), a2p, g2p, row(rwkv_k_k), row(rwkv_k_a),
        row(rwkv_r_k), e_blk)

    yn = _to_head_major(_rwkv_scan(_to_time_major(scan_in), tl["scan_tt"]))

    w_out_bf = w_out.astype(BF16)
    x1, h2, h8, sh = _outproj(x2, y_lru, yn, bonus, g_s, row(rwkv_lnx_w), row(rwkv_lnx_b),
                              w_out_bf[:c_lru], w_out_bf[c_lru:], row(norm2_g), tl["outproj_tm"])

    q = _qproj(h2, peer_wq.astype(BF16), tl["qproj_tm"])
    keys = peer_keys.reshape(2 * PEER_HEADS, PEER_NKEYS, -1)
    gates, ii, jj = _route(q, keys, tl["route_tm"])
    u8, su = _quantize_rows(peer_u, tl["quant_tr"])
    v8, sv = _quantize_rows(peer_v, tl["quant_tr"])
    w = _wbuild(gates, ii, jj, sv, tl["wbuild_tw"])
    gf = jnp.ones((1, d), F32) if final_g is None else final_g.reshape(1, d)
    return _peer(h8, sh, u8, su, v8, w, x1, gf, final_g is not None, tl["peer_tm"], tl["peer_te"])


def kernel(x, norm1_g, w_in, conv_w, conv_b, lru_wa, lru_ba, lru_wi, lru_bi, lru_lambda, rwkv_mu, rwkv_w0, rwkv_w2, rwkv_a0, rwkv_a2, rwkv_g2, rwkv_k_k, rwkv_k_a, rwkv_r_k, rwkv_lnx_w, rwkv_lnx_b, w_out, norm2_g, peer_wq, peer_keys, peer_u, peer_v, normf_g):
    b, s, d = x.shape
    depth = norm1_g.shape[0]
    x2 = x.reshape(b * s, d)
    for l in range(depth):
        final_g = normf_g if l == depth - 1 else None
        x2 = _layer(x2, b, s, final_g, norm1_g[l], w_in[l], conv_w[l], conv_b[l], lru_wa[l], lru_ba[l],
                    lru_wi[l], lru_bi[l], lru_lambda[l], rwkv_mu[l], rwkv_w0[l], rwkv_w2[l],
                    rwkv_a0[l], rwkv_a2[l], rwkv_g2[l], rwkv_k_k[l], rwkv_k_a[l],
                    rwkv_r_k[l].reshape(-1), rwkv_lnx_w[l], rwkv_lnx_b[l], w_out[l], norm2_g[l],
                    peer_wq[l], peer_keys[l], peer_u[l], peer_v[l])
    return x2.reshape(b, s, d)
```

```python
import functools

import jax
import jax.numpy as jnp
import numpy as np
from jax import lax
from jax.experimental import pallas as pl
from jax.experimental.pallas import tpu as pltpu

NORM_EPS = 1e-6
GN_EPS = 64e-5
LRU_C = 8.0
LRU_HEADS = 8
CONV_WIDTH = 4
RWKV_HEAD_DIM = 64
DECAY_LORA = 64
AAA_LORA = 64
GATE_LORA = 160
PEER_HEADS = 8
PEER_NKEYS = 128
PEER_TOPK = 16

LANES = 128
SUBLANES = 8
LORA_PAD = 512
W_PITCH = 136
N_PITCH = 72
N_SCAN_IN = 6
PEER_ROW_CHUNKS = 2
TOKEN_UNROLL = 32
VMEM_LIMIT = 48 * 1024 * 1024
PEER_VMEM_LIMIT = 56 * 1024 * 1024

F32 = jnp.float32
BF16 = jnp.bfloat16
FP8 = jnp.float8_e4m3fn
FP8_TARGET = 256.0
TINY = 1e-30


def _fp8_scale(amax):
    return jnp.maximum(amax, TINY) * (1.0 / FP8_TARGET)


def _params(n_axes):
    return pltpu.CompilerParams(dimension_semantics=("arbitrary",) * n_axes,
                                vmem_limit_bytes=VMEM_LIMIT)


def _softplus(z):
    return jnp.maximum(z, 0.0) + jnp.log1p(jnp.exp(-jnp.abs(z)))


def _gelu_tanh_bf16(x):
    c = float(np.float32(np.sqrt(2.0 / np.pi)))
    k = float(np.float32(0.044715))
    arg = x * (c + (c * k) * (x * x))
    hx = x.astype(BF16) * 0.5
    return hx + hx * jnp.tanh(arg.astype(BF16))


def _split_bf16(x):
    hi = x.astype(BF16)
    lo = (x - hi.astype(F32)).astype(BF16)
    return hi, lo


def _segsum(x, e_blk):
    outs = []
    for c in range(x.shape[1] // LANES):
        hi, lo = _split_bf16(x[:, c * LANES:(c + 1) * LANES])
        outs.append(jnp.dot(hi, e_blk, preferred_element_type=F32)
                    + jnp.dot(lo, e_blk, preferred_element_type=F32))
    return jnp.concatenate(outs, axis=1)


def _inproj_kernel(x_ref, g_ref, w_ref, o_ref, h_ref):
    @pl.when(pl.program_id(1) == 0)
    def _():
        x = x_ref[...]
        ms = jnp.mean(x * x, axis=-1, keepdims=True)
        h_ref[...] = (x * lax.rsqrt(ms + NORM_EPS) * g_ref[...]).astype(BF16)

    o_ref[...] = jnp.dot(h_ref[...], w_ref[...], preferred_element_type=F32)


def _inproj(x2, g, w_bf, tm, tn):
    t, d = x2.shape
    n = w_bf.shape[1]
    return pl.pallas_call(
        _inproj_kernel,
        grid=(t // tm, n // tn),
        in_specs=[pl.BlockSpec((tm, d), lambda i, j: (i, 0)),
                  pl.BlockSpec((1, d), lambda i, j: (0, 0)),
                  pl.BlockSpec((d, tn), lambda i, j: (0, j))],
        out_specs=pl.BlockSpec((tm, tn), lambda i, j: (i, j)),
        out_shape=jax.ShapeDtypeStruct((t, n), F32),
        scratch_shapes=[pltpu.VMEM((tm, d), BF16)],
        compiler_params=_params(2),
        name="inproj",
    )(x2, g, w_bf)


def _lru_kernel(xb_ref, gate_ref, cw_ref, cb_ref, wa_ref, ba_ref, wi_ref, bi_ref, lam_ref,
                o_ref, ext_ref, a_ref, b_ref, hcar_ref):
    tt, c = xb_ref.shape
    hd = c // LRU_HEADS

    @pl.when(pl.program_id(1) == 0)
    def _():
        ext_ref[0:SUBLANES, :] = jnp.zeros((SUBLANES, c), F32)
        hcar_ref[...] = jnp.zeros((1, c), F32)

    x = xb_ref[...]
    ext_ref[SUBLANES:SUBLANES + tt, :] = x
    xc = cb_ref[...] + cw_ref[CONV_WIDTH - 1:CONV_WIDTH, :] * x
    for k in range(1, CONV_WIDTH):
        xc = xc + cw_ref[CONV_WIDTH - 1 - k:CONV_WIDTH - k, :] * ext_ref[SUBLANES - k:SUBLANES - k + tt, :]
    ext_ref[0:SUBLANES, :] = x[tt - SUBLANES:tt, :]

    xcb = xc.astype(BF16)
    ra, ia = [], []
    for h in range(LRU_HEADS):
        xh = xcb[:, h * hd:(h + 1) * hd]
        ra.append(jnp.dot(xh, wa_ref[h], preferred_element_type=F32))
        ia.append(jnp.dot(xh, wi_ref[h], preferred_element_type=F32))
    r = jax.nn.sigmoid(jnp.concatenate(ra, axis=1) + ba_ref[...])
    ig = jax.nn.sigmoid(jnp.concatenate(ia, axis=1) + bi_ref[...])
    log_a = (-LRU_C) * r * _softplus(-lam_ref[...])
    a = jnp.exp(log_a)
    bx = jnp.sqrt(-jnp.tanh(log_a) * (a * a + 1.0)) * (ig * xc)

    row = lax.broadcasted_iota(jnp.int32, (tt, c), 0) % SUBLANES
    for d in (1, 2, 4):
        m = row >= d
        a_sh = jnp.where(m, pltpu.roll(a, d, 0), 1.0)
        b_sh = jnp.where(m, pltpu.roll(bx, d, 0), 0.0)
        bx = a * b_sh + bx
        a = a * a_sh
    a_ref[...] = a
    b_ref[...] = bx

    def blk(i, hc):
        off = pl.multiple_of(i * SUBLANES, SUBLANES)
        hb = a_ref[pl.ds(off, SUBLANES), :] * hc + b_ref[pl.ds(off, SUBLANES), :]
        b_ref[pl.ds(off, SUBLANES), :] = hb
        return hb[SUBLANES - 1:SUBLANES, :]

    hcar_ref[...] = lax.fori_loop(0, tt // SUBLANES, blk, hcar_ref[...])
    o_ref[...] = (b_ref[...] * jax.nn.gelu(gate_ref[...])).astype(BF16)


def _lru(proj, b, s, tt, cw, cb, wa, ba, wi, bi, lam):
    c = cw.shape[1]
    nt = s // tt
    row = lambda bb, ss: (bb * nt + ss, 0)
    const2 = lambda bb, ss: (0, 0)
    const3 = lambda bb, ss: (0, 0, 0)
    hd = c // LRU_HEADS
    return pl.pallas_call(
        _lru_kernel,
        grid=(b, nt),
        in_specs=[pl.BlockSpec((tt, c), row),
                  pl.BlockSpec((tt, c), lambda bb, ss: (bb * nt + ss, 1)),
                  pl.BlockSpec((CONV_WIDTH, c), const2),
                  pl.BlockSpec((1, c), const2),
                  pl.BlockSpec((LRU_HEADS, hd, hd), const3),
                  pl.BlockSpec((1, c), const2),
                  pl.BlockSpec((LRU_HEADS, hd, hd), const3),
                  pl.BlockSpec((1, c), const2),
                  pl.BlockSpec((1, c), const2)],
        out_specs=pl.BlockSpec((tt, c), row),
        out_shape=jax.ShapeDtypeStruct((b * s, c), BF16),
        scratch_shapes=[pltpu.VMEM((tt + SUBLANES, c), F32),
                        pltpu.VMEM((tt, c), F32),
                        pltpu.VMEM((tt, c), F32),
                        pltpu.VMEM((1, c), F32)],
        compiler_params=_params(2),
        name="rg_lru",
    )(proj, proj, cw, cb, wa, ba, wi, bi, lam)


def _token_shift(ext_ref, x, mu):
    tt = x.shape[0]
    ext_ref[SUBLANES:SUBLANES + tt, :] = x
    prev = ext_ref[SUBLANES - 1:SUBLANES - 1 + tt, :]
    ext_ref[0:SUBLANES, :] = x[tt - SUBLANES:tt, :]
    return x + (prev - x) * mu


def _store_head_major(o_ref, q, x):
    n = RWKV_HEAD_DIM
    per = LANES // n
    for cc in range(x.shape[1] // LANES):
        xt = x[:, cc * LANES:(cc + 1) * LANES].T
        for hh in range(per):
            h = per * cc + hh
            o_ref[q, h * N_PITCH:h * N_PITCH + n, :] = xt[hh * n:(hh + 1) * n, :]
            o_ref[q, h * N_PITCH + n:(h + 1) * N_PITCH, :] = jnp.zeros((N_PITCH - n, x.shape[0]), F32)


def _rwkv_prep_kernel(r_ref, k_ref, v_ref, l_ref, mur_ref, muk_ref, muv_ref, mul_ref,
                      w0_ref, w2_ref, a0_ref, a2_ref, g2_ref, kk_ref, ka_ref, rk_ref, e_ref,
                      so_ref, go_ref, bonus_ref,
                      er_ref, ek_ref, ev_ref, el_ref):
    @pl.when(pl.program_id(1) == 0)
    def _():
        for ext in (er_ref, ek_ref, ev_ref, el_ref):
            ext[0:SUBLANES, :] = jnp.zeros((SUBLANES, ext.shape[1]), F32)

    r = _token_shift(er_ref, r_ref[...], mur_ref[...])
    k = _token_shift(ek_ref, k_ref[...], muk_ref[...])
    v = _token_shift(ev_ref, v_ref[...], muv_ref[...])
    lo = _token_shift(el_ref, l_ref[...], mul_ref[...])

    lo_a = lo[:, 0:LANES]
    lo_g = lo[:, LANES:3 * LANES]
    wl = w0_ref[...] + jnp.dot(jnp.tanh(lo_a).astype(BF16), w2_ref[...], preferred_element_type=F32)
    w_log = -_softplus(-wl) - 0.5
    decay = jnp.exp(-jnp.exp(w_log))
    a = jax.nn.sigmoid(a0_ref[...] + jnp.dot(lo_a.astype(BF16), a2_ref[...], preferred_element_type=F32))
    g = jnp.dot(jax.nn.sigmoid(lo_g).astype(BF16), g2_ref[...], preferred_element_type=F32)

    e_blk = e_ref[...]
    kk = k * kk_ref[...]
    kk = kk / jnp.maximum(jnp.sqrt(_segsum(kk * kk, e_blk)), 1e-12)
    kmod = k * (1.0 + (a - 1.0) * ka_ref[...])

    for q, val in enumerate((r, decay, kmod, v, -kk, kk * a)):
        _store_head_major(so_ref, q, val)
    go_ref[...] = g
    bonus_ref[...] = _segsum(r * kmod * rk_ref[...], e_blk) * v


def _rwkv_prep(proj, b, s, tt, c, col0, mur, muk, muv, mul, w0, w2p, a0, a2p, g2p, kk, ka, rk, e_blk):
    nt = s // tt
    cb0 = col0 // c
    lb = (col0 + 3 * c) // LORA_PAD
    const2 = lambda bb, ss: (0, 0)
    vec = pl.BlockSpec((1, c), const2)
    row = lambda bb, ss: (bb * nt + ss, 0)
    out = jax.ShapeDtypeStruct((b * s, c), F32)
    hrows = (c // RWKV_HEAD_DIM) * N_PITCH
    return pl.pallas_call(
        _rwkv_prep_kernel,
        grid=(b, nt),
        in_specs=[pl.BlockSpec((tt, c), lambda bb, ss: (bb * nt + ss, cb0)),
                  pl.BlockSpec((tt, c), lambda bb, ss: (bb * nt + ss, cb0 + 1)),
                  pl.BlockSpec((tt, c), lambda bb, ss: (bb * nt + ss, cb0 + 2)),
                  pl.BlockSpec((tt, LORA_PAD), lambda bb, ss: (bb * nt + ss, lb)),
                  vec, vec, vec, pl.BlockSpec((1, LORA_PAD), const2),
                  vec, pl.BlockSpec((LANES, c), const2),
                  vec, pl.BlockSpec((LANES, c), const2),
                  pl.BlockSpec((2 * LANES, c), const2),
                  vec, vec, vec, pl.BlockSpec((LANES, LANES), const2)],
        out_specs=[pl.BlockSpec((N_SCAN_IN, hrows, tt), lambda bb, ss: (0, bb, ss)),
                   pl.BlockSpec((tt, c), row), pl.BlockSpec((tt, c), row)],
        out_shape=[jax.ShapeDtypeStruct((N_SCAN_IN, b * hrows, s), F32), out, out],
        scratch_shapes=[pltpu.VMEM((tt + SUBLANES, c), F32),
                        pltpu.VMEM((tt + SUBLANES, c), F32),
                        pltpu.VMEM((tt + SUBLANES, c), F32),
                        pltpu.VMEM((tt + SUBLANES, LORA_PAD), F32)],
        compiler_params=_params(2),
        name="rwkv_prep",
    )(proj, proj, proj, proj, mur, muk, muv, mul, w0, w2p, a0, a2p, g2p, kk, ka, rk, e_blk)


def _swap_kernel(i_ref, o_ref):
    for n in range(RWKV_HEAD_DIM):
        o_ref[pl.ds(n, LANES, stride=N_PITCH), :] = i_ref[pl.ds(n, LANES, stride=N_PITCH), :].T
    for n in range(RWKV_HEAD_DIM, N_PITCH):
        o_ref[pl.ds(n, LANES, stride=N_PITCH), :] = jnp.zeros((LANES, LANES), F32)


def _to_time_major(p):
    nq, rows, s = p.shape
    return pl.pallas_call(
        _swap_kernel,
        grid=(nq, s // LANES),
        in_specs=[pl.BlockSpec((None, rows, LANES), lambda q, i: (q, 0, i))],
        out_specs=pl.BlockSpec((None, rows, LANES), lambda q, i: (q, i, 0)),
        out_shape=jax.ShapeDtypeStruct((nq, s * N_PITCH, LANES), F32),
        compiler_params=_params(2),
        name="to_time_major",
    )(p)


def _to_head_major(y):
    rows_total, bh = y.shape
    s = rows_total // N_PITCH
    rows = LANES * N_PITCH
    return pl.pallas_call(
        _swap_kernel,
        grid=(s // LANES,),
        in_specs=[pl.BlockSpec((rows, LANES), lambda i: (i, 0))],
        out_specs=pl.BlockSpec((rows, LANES), lambda i: (0, i)),
        out_shape=jax.ShapeDtypeStruct((bh * N_PITCH, s), F32),
        compiler_params=_params(1),
        name="to_head_major",
    )(y)


def _rwkv_scan_kernel(q_ref, y_ref, s_ref, sa_ref):
    n = RWKV_HEAD_DIM
    tt = y_ref.shape[0] // N_PITCH

    @pl.when(pl.program_id(0) == 0)
    def _():
        s_ref[...] = jnp.zeros(s_ref.shape, F32)

    def step(t, carry):
        base = pl.multiple_of(t * N_PITCH, SUBLANES)
        r = q_ref[0, pl.ds(base, n), :]
        w = q_ref[1, pl.ds(base, n), :]
        k = q_ref[2, pl.ds(base, n), :]
        na = q_ref[4, pl.ds(base, n), :]
        b = q_ref[5, pl.ds(base, n), :]
        for i in range(n):
            sa_ref[i:i + 1, :] = jnp.sum(s_ref[i] * na, axis=0, keepdims=True)
        for i in range(n):
            s_new = s_ref[i] * w + sa_ref[i:i + 1, :] * b + q_ref[3, pl.ds(base + i, 1), :] * k
            s_ref[i] = s_new
            y_ref[pl.ds(base + i, 1), :] = jnp.sum(s_new * r, axis=0, keepdims=True)
        y = y_ref[pl.ds(base, n), :]
        mean = jnp.mean(y, axis=0, keepdims=True)
        yc = y - mean
        var = jnp.mean(yc * yc, axis=0, keepdims=True)
        y_ref[pl.ds(base, n), :] = yc * lax.rsqrt(var + GN_EPS)
        y_ref[pl.ds(base + n, N_PITCH - n), :] = jnp.zeros((N_PITCH - n, y_ref.shape[1]), F32)
        return carry

    lax.fori_loop(0, tt, step, 0)


def _rwkv_scan(q, tt):
    nq, rows_total, bh = q.shape
    s = rows_total // N_PITCH
    n = RWKV_HEAD_DIM
    return pl.pallas_call(
        _rwkv_scan_kernel,
        grid=(s // tt,),
        in_specs=[pl.BlockSpec((nq, tt * N_PITCH, bh), lambda i: (0, i, 0))],
        out_specs=pl.BlockSpec((tt * N_PITCH, bh), lambda i: (i, 0)),
        out_shape=jax.ShapeDtypeStruct((rows_total, bh), F32),
        scratch_shapes=[pltpu.VMEM((n, n, bh), F32), pltpu.VMEM((n, bh), F32)],
        compiler_params=_params(1),
        name="rwkv_scan",
    )(q)


def _outproj_kernel(x_ref, ylru_ref, yn_ref, bonus_ref, g_ref, lw_ref, lb_ref, wo1_ref, wo2_ref,
                    n2_ref, x1_ref, h2_ref, h8_ref, sh_ref):
    n = RWKV_HEAD_DIM
    per = LANES // n
    cols = []
    for cc in range(bonus_ref.shape[1] // LANES):
        blk = jnp.concatenate([yn_ref[(per * cc + hh) * N_PITCH:(per * cc + hh) * N_PITCH + n, :]
                               for hh in range(per)], axis=0)
        cols.append(blk.T)
    yn = jnp.concatenate(cols, axis=1)
    y_rwkv = ((yn * lw_ref[...] + lb_ref[...] + bonus_ref[...]) * g_ref[...]).astype(BF16)
    x1 = (x_ref[...]
          + jnp.dot(ylru_ref[...], wo1_ref[...], preferred_element_type=F32)
          + jnp.dot(y_rwkv, wo2_ref[...], preferred_element_type=F32))
    x1_ref[...] = x1
    ms = jnp.mean(x1 * x1, axis=-1, keepdims=True)
    h2 = x1 * lax.rsqrt(ms + NORM_EPS) * n2_ref[...]
    h2_ref[...] = h2.astype(BF16)
    sh = _fp8_scale(jnp.max(jnp.abs(h2), axis=-1, keepdims=True))
    sh_ref[...] = sh
    h8_ref[...] = (h2 / sh).astype(FP8)


def _outproj(x2, ylru, yn, bonus, g, lw, lb, wo1, wo2, n2, tm):
    t, d = x2.shape
    c = ylru.shape[1]
    s = yn.shape[1]
    nst = s // tm
    hrows = (c // RWKV_HEAD_DIM) * N_PITCH
    row_d = pl.BlockSpec((tm, d), lambda i: (i, 0))
    row_c = pl.BlockSpec((tm, c), lambda i: (i, 0))
    const = lambda i: (0, 0)
    return pl.pallas_call(
        _outproj_kernel,
        grid=(t // tm,),
        in_specs=[row_d, row_c, pl.BlockSpec((hrows, tm), lambda i: (i // nst, i % nst)), row_c, row_c,
                  pl.BlockSpec((1, c), const), pl.BlockSpec((1, c), const),
                  pl.BlockSpec((c, d), const), pl.BlockSpec((c, d), const),
                  pl.BlockSpec((1, d), const)],
        out_specs=[row_d, row_d, row_d, pl.BlockSpec((tm, 1), lambda i: (i, 0))],
        out_shape=[jax.ShapeDtypeStruct((t, d), F32), jax.ShapeDtypeStruct((t, d), BF16),
                   jax.ShapeDtypeStruct((t, d), FP8), jax.ShapeDtypeStruct((t, 1), F32)],
        compiler_params=_params(1),
        name="outproj",
    )(x2, ylru, yn, bonus, g, lw, lb, wo1, wo2, n2)


def _matmul_kernel(a_ref, w_ref, o_ref):
    o_ref[...] = jnp.dot(a_ref[...], w_ref[...], preferred_element_type=F32)


def _qproj(h2, wq, tm):
    t, d = h2.shape
    n = wq.shape[1]
    return pl.pallas_call(
        _matmul_kernel,
        grid=(t // tm,),
        in_specs=[pl.BlockSpec((tm, d), lambda i: (i, 0)),
                  pl.BlockSpec((d, n), lambda i: (0, 0))],
        out_specs=pl.BlockSpec((tm, n), lambda i: (i, 0)),
        out_shape=jax.ShapeDtypeStruct((t, n), F32),
        compiler_params=_params(1),
        name="qproj",
    )(h2, wq)


def _dot_nt3(a, b):
    dn = (((1,), (1,)), ((), ()))
    ah, al = _split_bf16(a)
    bh, bl = _split_bf16(b)
    return (lax.dot_general(ah, bh, dn, preferred_element_type=F32)
            + lax.dot_general(ah, bl, dn, preferred_element_type=F32)
            + lax.dot_general(al, bh, dn, preferred_element_type=F32))


def _tree(op, xs):
    xs = list(xs)
    while len(xs) > 1:
        nxt = [op(xs[i], xs[i + 1]) for i in range(0, len(xs) - 1, 2)]
        if len(xs) % 2:
            nxt.append(xs[-1])
        xs = nxt
    return xs[0]


def _route_kernel(q_ref, keys_ref, g_ref, ii_ref, jj_ref):
    tm = q_ref.shape[0]
    nk = PEER_NKEYS
    neg = -jnp.inf
    iota_n = lax.broadcasted_iota(jnp.int32, (nk, tm), 0).astype(F32)

    vals = [[None, None] for _ in range(PEER_HEADS)]
    idxs = [[None, None] for _ in range(PEER_HEADS)]
    for h in range(PEER_HEADS):
        for p in range(2):
            hp = 2 * h + p
            cur = _dot_nt3(keys_ref[hp], q_ref[:, hp * nk:(hp + 1) * nk])
            vs, ix = [], []
            for _ in range(PEER_TOPK):
                m = jnp.max(cur, axis=0, keepdims=True)
                idx = jnp.min(jnp.where(cur == m, iota_n, float(nk)), axis=0, keepdims=True)
                vs.append(m)
                ix.append(idx)
                cur = jnp.where(iota_n == idx, neg, cur)
            vals[h][p] = vs
            idxs[h][p] = ix

    def by_rank(table, p, rnk):
        return jnp.concatenate([table[h][p][rnk] for h in range(PEER_HEADS)], axis=0)

    va = [by_rank(vals, 0, r) for r in range(PEER_TOPK)]
    vb = [by_rank(vals, 1, r) for r in range(PEER_TOPK)]
    ia = [by_rank(idxs, 0, r) for r in range(PEER_TOPK)]
    ib = [by_rank(idxs, 1, r) for r in range(PEER_TOPK)]

    cand = []
    for p in range(PEER_TOPK):
        for qq in range(PEER_TOPK // (p + 1)):
            cand.append([va[p] + vb[qq], ia[p], ib[qq]])
    ncand = len(cand)

    best, sel_i, sel_j = [], [], []
    for _ in range(PEER_TOPK):
        m = _tree(jnp.maximum, [c[0] for c in cand])
        code = _tree(jnp.minimum, [jnp.where(c[0] == m, float(ci), float(ncand))
                                   for ci, c in enumerate(cand)])
        ci_sel = jnp.zeros_like(m)
        cj_sel = jnp.zeros_like(m)
        for ci, c in enumerate(cand):
            win = code == float(ci)
            ci_sel = jnp.where(win, c[1], ci_sel)
            cj_sel = jnp.where(win, c[2], cj_sel)
            c[0] = jnp.where(win, neg, c[0])
        best.append(m)
        sel_i.append(ci_sel)
        sel_j.append(cj_sel)

    ex = [jnp.exp(bv - best[0]) for bv in best]
    z = _tree(jnp.add, ex)
    gates = jnp.concatenate([e / z for e in ex], axis=0)
    g_ref[...] = gates.T
    ii_ref[...] = jnp.concatenate(sel_i, axis=0).T
    jj_ref[...] = jnp.concatenate(sel_j, axis=0).T


def _route(q, keys, tm):
    t, dq = q.shape
    nslot = PEER_TOPK * PEER_HEADS
    out = jax.ShapeDtypeStruct((t, nslot), F32)
    oblk = pl.BlockSpec((tm, nslot), lambda i: (i, 0))
    return pl.pallas_call(
        _route_kernel,
        grid=(t // tm,),
        in_specs=[pl.BlockSpec((tm, dq), lambda i: (i, 0)),
                  pl.BlockSpec(keys.shape, lambda i: (0, 0, 0))],
        out_specs=[oblk, oblk, oblk],
        out_shape=[out, out, out],
        compiler_params=_params(1),
        name="peer_route",
    )(q, keys)


def _wbuild_kernel(g_ref, ii_ref, jj_ref, sv_ref, w_ref, buf_ref):
    tw = g_ref.shape[0]
    nk = PEER_NKEYS
    iota_s = lax.broadcasted_iota(jnp.int32, (nk, g_ref.shape[1]), 0).astype(F32)
    dn = (((1,), (1,)), ((), ()))

    def token_block(t):
        g = g_ref[pl.ds(t, 1), :]
        a = jnp.where(iota_s == ii_ref[pl.ds(t, 1), :], g, 0.0).astype(BF16)
        bt = jnp.where(iota_s == jj_ref[pl.ds(t, 1), :], 1.0, 0.0).astype(BF16)
        return lax.dot_general(a, bt, dn, preferred_element_type=F32) * sv_ref[...]

    def tok(t, carry):
        off = pl.multiple_of(t * W_PITCH, SUBLANES)
        buf_ref[pl.ds(off, nk), :] = token_block(t)
        return carry

    lax.fori_loop(0, tw, tok, 0, unroll=TOKEN_UNROLL)
    per = w_ref.shape[2] // nk
    for i in range(nk):
        w_ref[i // per, :, (i % per) * nk:(i % per + 1) * nk] = (
            buf_ref[pl.ds(i, tw, stride=W_PITCH), :].astype(BF16))


def _wbuild(gates, ii, jj, sv, tw, te):
    t, nslot = gates.shape
    ne = PEER_NKEYS * PEER_NKEYS
    blk = pl.BlockSpec((tw, nslot), lambda i: (i, 0))
    return pl.pallas_call(
        _wbuild_kernel,
        grid=(t // tw,),
        in_specs=[blk, blk, blk, pl.BlockSpec((PEER_NKEYS, PEER_NKEYS), lambda i: (0, 0))],
        out_specs=pl.BlockSpec((ne // te, tw, te), lambda i: (0, i, 0)),
        out_shape=jax.ShapeDtypeStruct((ne // te, t, te), BF16),
        scratch_shapes=[pltpu.VMEM((tw * W_PITCH, PEER_NKEYS), F32)],
        compiler_params=_params(1),
        name="peer_weights",
    )(gates, ii, jj, sv.reshape(PEER_NKEYS, PEER_NKEYS))


def _peer_step(h_ref, sh_ref, u_ref, su_ref, v_ref, w_ref, o_ref,
               act_new, act_old, p_new, s_new, p_old, s_old):
    tm = h_ref.shape[0]
    rc = tm // PEER_ROW_CHUNKS
    for c in range(PEER_ROW_CHUNKS):
        rows = slice(c * rc, (c + 1) * rc)
        a = act_old[rows, :] * sh_ref[rows, :] * su_ref[...]
        p = _gelu_tanh_bf16(a) * w_ref[rows, :]
        sp = _fp8_scale(jnp.max(jnp.abs(p), axis=-1, keepdims=True).astype(F32))
        p_new[rows, :] = (p.astype(F32) / sp).astype(FP8)
        s_new[rows, :] = sp
        act_new[rows, :] = lax.dot_general(h_ref[rows, :], u_ref[...], (((1,), (1,)), ((), ())),
                                           preferred_element_type=F32)
        o_ref[rows, :] += jnp.dot(p_old[rows, :], v_ref[...], preferred_element_type=F32) * s_old[rows, :]


def _peer_kernel(h_ref, sh_ref, u_ref, su_ref, v_ref, w_ref, x1_ref, gf_ref, o_ref,
                 act0_ref, act1_ref, p0_ref, s0_ref, p1_ref, s1_ref, *, ne, final_norm):
    g = pl.program_id(0)
    e_out = (g - 2) % ne

    @pl.when(g == 0)
    def _():
        act1_ref[...] = jnp.zeros(act1_ref.shape, F32)
        p0_ref[...] = jnp.zeros(p0_ref.shape, FP8)
        s0_ref[...] = jnp.ones(s0_ref.shape, F32)

    @pl.when((g < 2) | (e_out == 0))
    def _():
        o_ref[...] = jnp.zeros(o_ref.shape, F32)

    args = (h_ref, sh_ref, u_ref, su_ref, v_ref, w_ref, o_ref)

    @pl.when(g % 2 == 0)
    def _():
        _peer_step(*args, act0_ref, act1_ref, p1_ref, s1_ref, p0_ref, s0_ref)

    @pl.when(g % 2 == 1)
    def _():
        _peer_step(*args, act1_ref, act0_ref, p0_ref, s0_ref, p1_ref, s1_ref)

    @pl.when((g >= 2) & (e_out == ne - 1))
    def _():
        x = x1_ref[...] + o_ref[...]
        if final_norm:
            ms = jnp.mean(x * x, axis=-1, keepdims=True)
            x = x * lax.rsqrt(ms + NORM_EPS) * gf_ref[...]
        o_ref[...] = x


def _peer(h8, sh, u8, su, v8, w, x1, gf, final_norm, tm, te):
    t, d = h8.shape
    ne = u8.shape[0] // te
    nt = t // tm
    tok = lambda g: jnp.clip(g // ne, 0, nt - 1)
    return pl.pallas_call(
        functools.partial(_peer_kernel, ne=ne, final_norm=final_norm),
        grid=(nt * ne + 2,),
        in_specs=[pl.BlockSpec((tm, d), lambda g: (tok(g), 0)),
                  pl.BlockSpec((tm, 1), lambda g: (tok(g - 1), 0)),
                  pl.BlockSpec((te, d), lambda g: (g % ne, 0)),
                  pl.BlockSpec((1, te), lambda g: (0, (g - 1) % ne)),
                  pl.BlockSpec((te, d), lambda g: ((g - 2) % ne, 0)),
                  pl.BlockSpec((None, tm, te), lambda g: ((g - 1) % ne, tok(g - 1), 0)),
                  pl.BlockSpec((tm, d), lambda g: (tok(g - 2), 0), pipeline_mode=pl.Buffered(1)),
                  pl.BlockSpec((1, d), lambda g: (0, 0))],
        out_specs=pl.BlockSpec((tm, d), lambda g: (tok(g - 2), 0)),
        out_shape=jax.ShapeDtypeStruct((t, d), F32),
        scratch_shapes=[pltpu.VMEM((tm, te), F32), pltpu.VMEM((tm, te), F32),
                        pltpu.VMEM((tm, te), FP8), pltpu.VMEM((tm, 1), F32),
                        pltpu.VMEM((tm, te), FP8), pltpu.VMEM((tm, 1), F32)],
        compiler_params=pltpu.CompilerParams(dimension_semantics=("arbitrary",),
                                             vmem_limit_bytes=PEER_VMEM_LIMIT),
        name="peer_experts",
    )(h8, sh, u8, su, v8, w, x1, gf)


def _quantize_kernel(t_ref, q_ref, s_ref):
    x = t_ref[...]
    scale = _fp8_scale(jnp.max(jnp.abs(x), axis=1, keepdims=True))
    s_ref[...] = scale
    q_ref[...] = (x / scale).astype(FP8)


def _quantize_rows(tab, tr):
    n, d = tab.shape
    q, scale = pl.pallas_call(
        _quantize_kernel,
        grid=(n // tr,),
        in_specs=[pl.BlockSpec((tr, d), lambda i: (i, 0))],
        out_specs=[pl.BlockSpec((tr, d), lambda i: (i, 0)), pl.BlockSpec((tr, 1), lambda i: (i, 0))],
        out_shape=[jax.ShapeDtypeStruct((n, d), FP8), jax.ShapeDtypeStruct((n, 1), F32)],
        compiler_params=_params(1),
        name="quantize_rows",
    )(tab)
    return q, scale.reshape(1, n)


def _tiles(t, s):
    return dict(
        inproj_tm=min(1024, t), inproj_tn=512,
        seq_tt=min(256, s),
        scan_tt=min(32, s),
        outproj_tm=min(256, s),
        qproj_tm=min(512, t),
        route_tm=min(128, t),
        wbuild_tw=min(128, t),
        peer_tm=min(1024, t), peer_te=512,
        quant_tr=512,
    )


def _layer(x2, b, s, final_g, norm1_g, w_in, conv_w, conv_b, lru_wa, lru_ba, lru_wi, lru_bi, lru_lambda,
           rwkv_mu, rwkv_w0, rwkv_w2, rwkv_a0, rwkv_a2, rwkv_g2, rwkv_k_k, rwkv_k_a, rwkv_r_k,
           rwkv_lnx_w, rwkv_lnx_b, w_out, norm2_g, peer_wq, peer_keys, peer_u, peer_v):
    t, d = x2.shape
    c_lru = conv_w.shape[1]
    c_rwkv = rwkv_w0.shape[0]
    n_heads = c_rwkv // RWKV_HEAD_DIM
    tl = _tiles(t, s)
    row = lambda v: v.reshape(1, -1)

    n_main = 2 * c_lru + 3 * c_rwkv
    n_lora = DECAY_LORA + AAA_LORA + GATE_LORA
    w_in_p = jnp.pad(w_in, ((0, 0), (0, LORA_PAD - n_lora))).astype(BF16)
    proj = _inproj(x2, row(norm1_g), w_in_p, tl["inproj_tm"], tl["inproj_tn"])

    y_lru = _lru(proj, b, s, tl["seq_tt"], conv_w, row(conv_b), lru_wa.astype(BF16), row(lru_ba),
                 lru_wi.astype(BF16), row(lru_bi), row(lru_lambda))

    mu_main, mu_lora = rwkv_mu[:3 * c_rwkv], rwkv_mu[3 * c_rwkv:]
    mu_l = jnp.pad(mu_lora, (0, LORA_PAD - n_lora))
    w2p = jnp.pad(rwkv_w2, ((0, LANES - DECAY_LORA), (0, 0))).astype(BF16)
    a2p = jnp.pad(rwkv_a2, ((DECAY_LORA, LANES - DECAY_LORA - AAA_LORA), (0, 0))).astype(BF16)
    g2p = jnp.pad(rwkv_g2, ((0, 2 * LANES - GATE_LORA), (0, 0))).astype(BF16)
    seg = np.arange(LANES) // RWKV_HEAD_DIM
    e_blk = jnp.asarray(seg[:, None] == seg[None, :], dtype=BF16)
    assert b * n_heads == LANES, "the scan maps (batch, head) onto the 128 lanes"
    scan_in, g_s, bonus = _rwkv_prep(
        proj, b, s, tl["seq_tt"], c_rwkv, 2 * c_lru,
        row(mu_main[:c_rwkv]), row(mu_main[c_rwkv:2 * c_rwkv]), row(mu_main[2 * c_rwkv:]), row(mu_l),
        row(rwkv_w0), w2p, row(rwkv_a0), a2p, g2p, row(rwkv_k_k), row(rwkv_k_a),
        row(rwkv_r_k), e_blk)

    yn = _to_head_major(_rwkv_scan(_to_time_major(scan_in), tl["scan_tt"]))

    w_out_bf = w_out.astype(BF16)
    x1, h2, h8, sh = _outproj(x2, y_lru, yn, bonus, g_s, row(rwkv_lnx_w), row(rwkv_lnx_b),
                              w_out_bf[:c_lru], w_out_bf[c_lru:], row(norm2_g), tl["outproj_tm"])

    q = _qproj(h2, peer_wq.astype(BF16), tl["qproj_tm"])
    keys = peer_keys.reshape(2 * PEER_HEADS, PEER_NKEYS, -1)
    gates, ii, jj = _route(q, keys, tl["route_tm"])
    u8, su = _quantize_rows(peer_u, tl["quant_tr"])
    v8, sv = _quantize_rows(peer_v, tl["quant_tr"])
    w = _wbuild(gates, ii, jj, sv, tl["wbuild_tw"], tl["peer_te"])
    gf = jnp.ones((1, d), F32) if final_g is None else final_g.reshape(1, d)
    return _peer(h8, sh, u8, su, v8, w, x1, gf, final_g is not None, tl["peer_tm"], tl["peer_te"])


def kernel(x, norm1_g, w_in, conv_w, conv_b, lru_wa, lru_ba, lru_wi, lru_bi, lru_lambda, rwkv_mu, rwkv_w0, rwkv_w2, rwkv_a0, rwkv_a2, rwkv_g2, rwkv_k_k, rwkv_k_a, rwkv_r_k, rwkv_lnx_w, rwkv_lnx_b, w_out, norm2_g, peer_wq, peer_keys, peer_u, peer_v, normf_g):
    b, s, d = x.shape
    depth = norm1_g.shape[0]
    x2 = x.reshape(b * s, d)
    for l in range(depth):
        final_g = normf_g if l == depth - 1 else None
        x2 = _layer(x2, b, s, final_g, norm1_g[l], w_in[l], conv_w[l], conv_b[l], lru_wa[l], lru_ba[l],
                    lru_wi[l], lru_bi[l], lru_lambda[l], rwkv_mu[l], rwkv_w0[l], rwkv_w2[l],
                    rwkv_a0[l], rwkv_a2[l], rwkv_g2[l], rwkv_k_k[l], rwkv_k_a[l],
                    rwkv_r_k[l].reshape(-1), rwkv_lnx_w[l], rwkv_lnx_b[l], w_out[l], norm2_g[l],
                    peer_wq[l], peer_keys[l], peer_u[l], peer_v[l])
    return x2.reshape(b, s, d)
```

```python
import functools

import jax
import jax.numpy as jnp
import numpy as np
from jax import lax
from jax.experimental import pallas as pl
from jax.experimental.pallas import tpu as pltpu

NORM_EPS = 1e-6
GN_EPS = 64e-5
LRU_C = 8.0
LRU_HEADS = 8
CONV_WIDTH = 4
RWKV_HEAD_DIM = 64
DECAY_LORA = 64
AAA_LORA = 64
GATE_LORA = 160
PEER_HEADS = 8
PEER_NKEYS = 128
PEER_TOPK = 16

LANES = 128
SUBLANES = 8
LORA_PAD = 512
W_PITCH = 136
N_PITCH = 72
N_SCAN_IN = 6
PEER_ROW_CHUNKS = 2
TOKEN_UNROLL = 32
VMEM_LIMIT = 48 * 1024 * 1024
PEER_VMEM_LIMIT = 56 * 1024 * 1024

F32 = jnp.float32
BF16 = jnp.bfloat16
FP8 = jnp.float8_e4m3fn
FP8_TARGET = 256.0
TINY = 1e-30


def _fp8_scale(amax):
    return jnp.maximum(amax, TINY) * (1.0 / FP8_TARGET)


def _params(n_axes):
    return pltpu.CompilerParams(dimension_semantics=("arbitrary",) * n_axes,
                                vmem_limit_bytes=VMEM_LIMIT)


def _softplus(z):
    return jnp.maximum(z, 0.0) + jnp.log1p(jnp.exp(-jnp.abs(z)))


def _gelu_tanh_bf16(x):
    c = float(np.float32(np.sqrt(2.0 / np.pi)))
    k = float(np.float32(0.044715))
    arg = x * (c + (c * k) * (x * x))
    hx = x.astype(BF16) * 0.5
    return hx + hx * jnp.tanh(arg.astype(BF16))


def _split_bf16(x):
    hi = x.astype(BF16)
    lo = (x - hi.astype(F32)).astype(BF16)
    return hi, lo


def _segsum(x, e_blk):
    outs = []
    for c in range(x.shape[1] // LANES):
        hi, lo = _split_bf16(x[:, c * LANES:(c + 1) * LANES])
        outs.append(jnp.dot(hi, e_blk, preferred_element_type=F32)
                    + jnp.dot(lo, e_blk, preferred_element_type=F32))
    return jnp.concatenate(outs, axis=1)


def _inproj_kernel(x_ref, g_ref, w_ref, o_ref, h_ref):
    @pl.when(pl.program_id(1) == 0)
    def _():
        x = x_ref[...]
        ms = jnp.mean(x * x, axis=-1, keepdims=True)
        h_ref[...] = (x * lax.rsqrt(ms + NORM_EPS) * g_ref[...]).astype(BF16)

    o_ref[...] = jnp.dot(h_ref[...], w_ref[...], preferred_element_type=F32).astype(o_ref.dtype)


def _inproj(x2, g, w_bf, tm, tn):
    t, d = x2.shape
    n = w_bf.shape[1]
    return pl.pallas_call(
        _inproj_kernel,
        grid=(t // tm, n // tn),
        in_specs=[pl.BlockSpec((tm, d), lambda i, j: (i, 0)),
                  pl.BlockSpec((1, d), lambda i, j: (0, 0)),
                  pl.BlockSpec((d, tn), lambda i, j: (0, j))],
        out_specs=pl.BlockSpec((tm, tn), lambda i, j: (i, j)),
        out_shape=jax.ShapeDtypeStruct((t, n), BF16),
        scratch_shapes=[pltpu.VMEM((tm, d), BF16)],
        compiler_params=_params(2),
        name="inproj",
    )(x2, g, w_bf)


def _lru_kernel(xb_ref, gate_ref, cw_ref, cb_ref, wa_ref, ba_ref, wi_ref, bi_ref, lam_ref,
                o_ref, ext_ref, a_ref, b_ref, hcar_ref):
    tt, c = xb_ref.shape
    hd = c // LRU_HEADS

    @pl.when(pl.program_id(1) == 0)
    def _():
        ext_ref[0:SUBLANES, :] = jnp.zeros((SUBLANES, c), F32)
        hcar_ref[...] = jnp.zeros((1, c), F32)

    x = xb_ref[...].astype(F32)
    ext_ref[SUBLANES:SUBLANES + tt, :] = x
    xc = cb_ref[...] + cw_ref[CONV_WIDTH - 1:CONV_WIDTH, :] * x
    for k in range(1, CONV_WIDTH):
        xc = xc + cw_ref[CONV_WIDTH - 1 - k:CONV_WIDTH - k, :] * ext_ref[SUBLANES - k:SUBLANES - k + tt, :]
    ext_ref[0:SUBLANES, :] = x[tt - SUBLANES:tt, :]

    xcb = xc.astype(BF16)
    ra, ia = [], []
    for h in range(LRU_HEADS):
        xh = xcb[:, h * hd:(h + 1) * hd]
        ra.append(jnp.dot(xh, wa_ref[h], preferred_element_type=F32))
        ia.append(jnp.dot(xh, wi_ref[h], preferred_element_type=F32))
    r = jax.nn.sigmoid(jnp.concatenate(ra, axis=1) + ba_ref[...])
    ig = jax.nn.sigmoid(jnp.concatenate(ia, axis=1) + bi_ref[...])
    log_a = (-LRU_C) * r * _softplus(-lam_ref[...])
    a = jnp.exp(log_a)
    bx = jnp.sqrt(-jnp.tanh(log_a) * (a * a + 1.0)) * (ig * xc)

    row = lax.broadcasted_iota(jnp.int32, (tt, c), 0) % SUBLANES
    for d in (1, 2, 4):
        m = row >= d
        a_sh = jnp.where(m, pltpu.roll(a, d, 0), 1.0)
        b_sh = jnp.where(m, pltpu.roll(bx, d, 0), 0.0)
        bx = a * b_sh + bx
        a = a * a_sh
    a_ref[...] = a
    b_ref[...] = bx

    def blk(i, hc):
        off = pl.multiple_of(i * SUBLANES, SUBLANES)
        hb = a_ref[pl.ds(off, SUBLANES), :] * hc + b_ref[pl.ds(off, SUBLANES), :]
        b_ref[pl.ds(off, SUBLANES), :] = hb
        return hb[SUBLANES - 1:SUBLANES, :]

    hcar_ref[...] = lax.fori_loop(0, tt // SUBLANES, blk, hcar_ref[...])
    o_ref[...] = (b_ref[...] * jax.nn.gelu(gate_ref[...].astype(F32))).astype(BF16)


def _lru(proj, b, s, tt, cw, cb, wa, ba, wi, bi, lam):
    c = cw.shape[1]
    nt = s // tt
    row = lambda bb, ss: (bb * nt + ss, 0)
    const2 = lambda bb, ss: (0, 0)
    const3 = lambda bb, ss: (0, 0, 0)
    hd = c // LRU_HEADS
    return pl.pallas_call(
        _lru_kernel,
        grid=(b, nt),
        in_specs=[pl.BlockSpec((tt, c), row),
                  pl.BlockSpec((tt, c), lambda bb, ss: (bb * nt + ss, 1)),
                  pl.BlockSpec((CONV_WIDTH, c), const2),
                  pl.BlockSpec((1, c), const2),
                  pl.BlockSpec((LRU_HEADS, hd, hd), const3),
                  pl.BlockSpec((1, c), const2),
                  pl.BlockSpec((LRU_HEADS, hd, hd), const3),
                  pl.BlockSpec((1, c), const2),
                  pl.BlockSpec((1, c), const2)],
        out_specs=pl.BlockSpec((tt, c), row),
        out_shape=jax.ShapeDtypeStruct((b * s, c), BF16),
        scratch_shapes=[pltpu.VMEM((tt + SUBLANES, c), F32),
                        pltpu.VMEM((tt, c), F32),
                        pltpu.VMEM((tt, c), F32),
                        pltpu.VMEM((1, c), F32)],
        compiler_params=_params(2),
        name="rg_lru",
    )(proj, proj, cw, cb, wa, ba, wi, bi, lam)


def _token_shift(ext_ref, x, mu):
    tt = x.shape[0]
    ext_ref[SUBLANES:SUBLANES + tt, :] = x
    prev = ext_ref[SUBLANES - 1:SUBLANES - 1 + tt, :]
    ext_ref[0:SUBLANES, :] = x[tt - SUBLANES:tt, :]
    return x + (prev - x) * mu


def _store_head_major(o_ref, q, x):
    n = RWKV_HEAD_DIM
    per = LANES // n
    for cc in range(x.shape[1] // LANES):
        xt = x[:, cc * LANES:(cc + 1) * LANES].T
        for hh in range(per):
            h = per * cc + hh
            o_ref[q, h * N_PITCH:h * N_PITCH + n, :] = xt[hh * n:(hh + 1) * n, :]
            o_ref[q, h * N_PITCH + n:(h + 1) * N_PITCH, :] = jnp.zeros((N_PITCH - n, x.shape[0]), F32)


def _rwkv_prep_kernel(r_ref, k_ref, v_ref, l_ref, mur_ref, muk_ref, muv_ref, mul_ref,
                      w0_ref, w2_ref, a0_ref, a2_ref, g2_ref, kk_ref, ka_ref, rk_ref, e_ref,
                      so_ref, go_ref, bonus_ref,
                      er_ref, ek_ref, ev_ref, el_ref):
    @pl.when(pl.program_id(1) == 0)
    def _():
        for ext in (er_ref, ek_ref, ev_ref, el_ref):
            ext[0:SUBLANES, :] = jnp.zeros((SUBLANES, ext.shape[1]), F32)

    r = _token_shift(er_ref, r_ref[...].astype(F32), mur_ref[...])
    k = _token_shift(ek_ref, k_ref[...].astype(F32), muk_ref[...])
    v = _token_shift(ev_ref, v_ref[...].astype(F32), muv_ref[...])
    lo = _token_shift(el_ref, l_ref[...].astype(F32), mul_ref[...])

    lo_a = lo[:, 0:LANES]
    lo_g = lo[:, LANES:3 * LANES]
    wl = w0_ref[...] + jnp.dot(jnp.tanh(lo_a).astype(BF16), w2_ref[...], preferred_element_type=F32)
    w_log = -_softplus(-wl) - 0.5
    decay = jnp.exp(-jnp.exp(w_log))
    a = jax.nn.sigmoid(a0_ref[...] + jnp.dot(lo_a.astype(BF16), a2_ref[...], preferred_element_type=F32))
    g = jnp.dot(jax.nn.sigmoid(lo_g).astype(BF16), g2_ref[...], preferred_element_type=F32)

    e_blk = e_ref[...]
    kk = k * kk_ref[...]
    kk = kk / jnp.maximum(jnp.sqrt(_segsum(kk * kk, e_blk)), 1e-12)
    kmod = k * (1.0 + (a - 1.0) * ka_ref[...])

    for q, val in enumerate((r, decay, kmod, v, -kk, kk * a)):
        _store_head_major(so_ref, q, val)
    go_ref[...] = g
    bonus_ref[...] = _segsum(r * kmod * rk_ref[...], e_blk) * v


def _rwkv_prep(proj, b, s, tt, c, col0, mur, muk, muv, mul, w0, w2p, a0, a2p, g2p, kk, ka, rk, e_blk):
    nt = s // tt
    cb0 = col0 // c
    lb = (col0 + 3 * c) // LORA_PAD
    const2 = lambda bb, ss: (0, 0)
    vec = pl.BlockSpec((1, c), const2)
    row = lambda bb, ss: (bb * nt + ss, 0)
    out = jax.ShapeDtypeStruct((b * s, c), F32)
    hrows = (c // RWKV_HEAD_DIM) * N_PITCH
    return pl.pallas_call(
        _rwkv_prep_kernel,
        grid=(b, nt),
        in_specs=[pl.BlockSpec((tt, c), lambda bb, ss: (bb * nt + ss, cb0)),
                  pl.BlockSpec((tt, c), lambda bb, ss: (bb * nt + ss, cb0 + 1)),
                  pl.BlockSpec((tt, c), lambda bb, ss: (bb * nt + ss, cb0 + 2)),
                  pl.BlockSpec((tt, LORA_PAD), lambda bb, ss: (bb * nt + ss, lb)),
                  vec, vec, vec, pl.BlockSpec((1, LORA_PAD), const2),
                  vec, pl.BlockSpec((LANES, c), const2),
                  vec, pl.BlockSpec((LANES, c), const2),
                  pl.BlockSpec((2 * LANES, c), const2),
                  vec, vec, vec, pl.BlockSpec((LANES, LANES), const2)],
        out_specs=[pl.BlockSpec((N_SCAN_IN, hrows, tt), lambda bb, ss: (0, bb, ss)),
                   pl.BlockSpec((tt, c), row), pl.BlockSpec((tt, c), row)],
        out_shape=[jax.ShapeDtypeStruct((N_SCAN_IN, b * hrows, s), F32), out, out],
        scratch_shapes=[pltpu.VMEM((tt + SUBLANES, c), F32),
                        pltpu.VMEM((tt + SUBLANES, c), F32),
                        pltpu.VMEM((tt + SUBLANES, c), F32),
                        pltpu.VMEM((tt + SUBLANES, LORA_PAD), F32)],
        compiler_params=_params(2),
        name="rwkv_prep",
    )(proj, proj, proj, proj, mur, muk, muv, mul, w0, w2p, a0, a2p, g2p, kk, ka, rk, e_blk)


def _swap_kernel(i_ref, o_ref):
    for n in range(RWKV_HEAD_DIM):
        o_ref[pl.ds(n, LANES, stride=N_PITCH), :] = i_ref[pl.ds(n, LANES, stride=N_PITCH), :].T
    for n in range(RWKV_HEAD_DIM, N_PITCH):
        o_ref[pl.ds(n, LANES, stride=N_PITCH), :] = jnp.zeros((LANES, LANES), F32)


def _to_time_major(p):
    nq, rows, s = p.shape
    return pl.pallas_call(
        _swap_kernel,
        grid=(nq, s // LANES),
        in_specs=[pl.BlockSpec((None, rows, LANES), lambda q, i: (q, 0, i))],
        out_specs=pl.BlockSpec((None, rows, LANES), lambda q, i: (q, i, 0)),
        out_shape=jax.ShapeDtypeStruct((nq, s * N_PITCH, LANES), F32),
        compiler_params=_params(2),
        name="to_time_major",
    )(p)


def _to_head_major(y):
    rows_total, bh = y.shape
    s = rows_total // N_PITCH
    rows = LANES * N_PITCH
    return pl.pallas_call(
        _swap_kernel,
        grid=(s // LANES,),
        in_specs=[pl.BlockSpec((rows, LANES), lambda i: (i, 0))],
        out_specs=pl.BlockSpec((rows, LANES), lambda i: (0, i)),
        out_shape=jax.ShapeDtypeStruct((bh * N_PITCH, s), F32),
        compiler_params=_params(1),
        name="to_head_major",
    )(y)


def _rwkv_scan_kernel(q_ref, y_ref, s_ref, sa_ref):
    n = RWKV_HEAD_DIM
    tt = y_ref.shape[0] // N_PITCH

    @pl.when(pl.program_id(0) == 0)
    def _():
        s_ref[...] = jnp.zeros(s_ref.shape, F32)

    def step(t, carry):
        base = pl.multiple_of(t * N_PITCH, SUBLANES)
        r = q_ref[0, pl.ds(base, n), :]
        w = q_ref[1, pl.ds(base, n), :]
        k = q_ref[2, pl.ds(base, n), :]
        na = q_ref[4, pl.ds(base, n), :]
        b = q_ref[5, pl.ds(base, n), :]
        for i in range(n):
            sa_ref[i:i + 1, :] = jnp.sum(s_ref[i] * na, axis=0, keepdims=True)
        for i in range(n):
            s_new = s_ref[i] * w + sa_ref[i:i + 1, :] * b + q_ref[3, pl.ds(base + i, 1), :] * k
            s_ref[i] = s_new
            y_ref[pl.ds(base + i, 1), :] = jnp.sum(s_new * r, axis=0, keepdims=True)
        y = y_ref[pl.ds(base, n), :]
        mean = jnp.mean(y, axis=0, keepdims=True)
        yc = y - mean
        var = jnp.mean(yc * yc, axis=0, keepdims=True)
        y_ref[pl.ds(base, n), :] = yc * lax.rsqrt(var + GN_EPS)
        y_ref[pl.ds(base + n, N_PITCH - n), :] = jnp.zeros((N_PITCH - n, y_ref.shape[1]), F32)
        return carry

    lax.fori_loop(0, tt, step, 0)


def _rwkv_scan(q, tt):
    nq, rows_total, bh = q.shape
    s = rows_total // N_PITCH
    n = RWKV_HEAD_DIM
    return pl.pallas_call(
        _rwkv_scan_kernel,
        grid=(s // tt,),
        in_specs=[pl.BlockSpec((nq, tt * N_PITCH, bh), lambda i: (0, i, 0))],
        out_specs=pl.BlockSpec((tt * N_PITCH, bh), lambda i: (i, 0)),
        out_shape=jax.ShapeDtypeStruct((rows_total, bh), F32),
        scratch_shapes=[pltpu.VMEM((n, n, bh), F32), pltpu.VMEM((n, bh), F32)],
        compiler_params=_params(1),
        name="rwkv_scan",
    )(q)


def _outproj_kernel(x_ref, ylru_ref, yn_ref, bonus_ref, g_ref, lw_ref, lb_ref, wo1_ref, wo2_ref,
                    n2_ref, x1_ref, h2_ref, h8_ref, sh_ref):
    n = RWKV_HEAD_DIM
    per = LANES // n
    cols = []
    for cc in range(bonus_ref.shape[1] // LANES):
        blk = jnp.concatenate([yn_ref[(per * cc + hh) * N_PITCH:(per * cc + hh) * N_PITCH + n, :]
                               for hh in range(per)], axis=0)
        cols.append(blk.T)
    yn = jnp.concatenate(cols, axis=1)
    y_rwkv = ((yn * lw_ref[...] + lb_ref[...] + bonus_ref[...]) * g_ref[...]).astype(BF16)
    x1 = (x_ref[...]
          + jnp.dot(ylru_ref[...], wo1_ref[...], preferred_element_type=F32)
          + jnp.dot(y_rwkv, wo2_ref[...], preferred_element_type=F32))
    x1_ref[...] = x1
    ms = jnp.mean(x1 * x1, axis=-1, keepdims=True)
    h2 = x1 * lax.rsqrt(ms + NORM_EPS) * n2_ref[...]
    h2_ref[...] = h2.astype(BF16)
    sh = _fp8_scale(jnp.max(jnp.abs(h2), axis=-1, keepdims=True))
    sh_ref[...] = sh
    h8_ref[...] = (h2 / sh).astype(FP8)


def _outproj(x2, ylru, yn, bonus, g, lw, lb, wo1, wo2, n2, tm):
    t, d = x2.shape
    c = ylru.shape[1]
    s = yn.shape[1]
    nst = s // tm
    hrows = (c // RWKV_HEAD_DIM) * N_PITCH
    row_d = pl.BlockSpec((tm, d), lambda i: (i, 0))
    row_c = pl.BlockSpec((tm, c), lambda i: (i, 0))
    const = lambda i: (0, 0)
    return pl.pallas_call(
        _outproj_kernel,
        grid=(t // tm,),
        in_specs=[row_d, row_c, pl.BlockSpec((hrows, tm), lambda i: (i // nst, i % nst)), row_c, row_c,
                  pl.BlockSpec((1, c), const), pl.BlockSpec((1, c), const),
                  pl.BlockSpec((c, d), const), pl.BlockSpec((c, d), const),
                  pl.BlockSpec((1, d), const)],
        out_specs=[row_d, row_d, row_d, pl.BlockSpec((tm, 1), lambda i: (i, 0))],
        out_shape=[jax.ShapeDtypeStruct((t, d), F32), jax.ShapeDtypeStruct((t, d), BF16),
                   jax.ShapeDtypeStruct((t, d), FP8), jax.ShapeDtypeStruct((t, 1), F32)],
        compiler_params=_params(1),
        name="outproj",
    )(x2, ylru, yn, bonus, g, lw, lb, wo1, wo2, n2)


def _matmul_kernel(a_ref, w_ref, o_ref):
    o_ref[...] = jnp.dot(a_ref[...], w_ref[...], preferred_element_type=F32)


def _qproj(h2, wq, tm):
    t, d = h2.shape
    n = wq.shape[1]
    return pl.pallas_call(
        _matmul_kernel,
        grid=(t // tm,),
        in_specs=[pl.BlockSpec((tm, d), lambda i: (i, 0)),
                  pl.BlockSpec((d, n), lambda i: (0, 0))],
        out_specs=pl.BlockSpec((tm, n), lambda i: (i, 0)),
        out_shape=jax.ShapeDtypeStruct((t, n), F32),
        compiler_params=_params(1),
        name="qproj",
    )(h2, wq)


def _dot_nt3(a, b):
    dn = (((1,), (1,)), ((), ()))
    ah, al = _split_bf16(a)
    bh, bl = _split_bf16(b)
    return (lax.dot_general(ah, bh, dn, preferred_element_type=F32)
            + lax.dot_general(ah, bl, dn, preferred_element_type=F32)
            + lax.dot_general(al, bh, dn, preferred_element_type=F32))


def _tree(op, xs):
    xs = list(xs)
    while len(xs) > 1:
        nxt = [op(xs[i], xs[i + 1]) for i in range(0, len(xs) - 1, 2)]
        if len(xs) % 2:
            nxt.append(xs[-1])
        xs = nxt
    return xs[0]


def _route_kernel(q_ref, keys_ref, g_ref, ii_ref, jj_ref):
    tm = q_ref.shape[0]
    nk = PEER_NKEYS
    neg = -jnp.inf
    iota_n = lax.broadcasted_iota(jnp.int32, (nk, tm), 0).astype(F32)

    vals = [[None, None] for _ in range(PEER_HEADS)]
    idxs = [[None, None] for _ in range(PEER_HEADS)]
    for h in range(PEER_HEADS):
        for p in range(2):
            hp = 2 * h + p
            cur = _dot_nt3(keys_ref[hp], q_ref[:, hp * nk:(hp + 1) * nk])
            vs, ix = [], []
            for _ in range(PEER_TOPK):
                m = jnp.max(cur, axis=0, keepdims=True)
                idx = jnp.min(jnp.where(cur == m, iota_n, float(nk)), axis=0, keepdims=True)
                vs.append(m)
                ix.append(idx)
                cur = jnp.where(iota_n == idx, neg, cur)
            vals[h][p] = vs
            idxs[h][p] = ix

    def by_rank(table, p, rnk):
        return jnp.concatenate([table[h][p][rnk] for h in range(PEER_HEADS)], axis=0)

    va = [by_rank(vals, 0, r) for r in range(PEER_TOPK)]
    vb = [by_rank(vals, 1, r) for r in range(PEER_TOPK)]
    ia = [by_rank(idxs, 0, r) for r in range(PEER_TOPK)]
    ib = [by_rank(idxs, 1, r) for r in range(PEER_TOPK)]

    cand = []
    for p in range(PEER_TOPK):
        for qq in range(PEER_TOPK // (p + 1)):
            cand.append([va[p] + vb[qq], ia[p], ib[qq]])
    ncand = len(cand)

    best, sel_i, sel_j = [], [], []
    for _ in range(PEER_TOPK):
        m = _tree(jnp.maximum, [c[0] for c in cand])
        code = _tree(jnp.minimum, [jnp.where(c[0] == m, float(ci), float(ncand))
                                   for ci, c in enumerate(cand)])
        ci_sel = jnp.zeros_like(m)
        cj_sel = jnp.zeros_like(m)
        for ci, c in enumerate(cand):
            win = code == float(ci)
            ci_sel = jnp.where(win, c[1], ci_sel)
            cj_sel = jnp.where(win, c[2], cj_sel)
            c[0] = jnp.where(win, neg, c[0])
        best.append(m)
        sel_i.append(ci_sel)
        sel_j.append(cj_sel)

    ex = [jnp.exp(bv - best[0]) for bv in best]
    z = _tree(jnp.add, ex)
    gates = jnp.concatenate([e / z for e in ex], axis=0)
    g_ref[...] = gates.T
    ii_ref[...] = jnp.concatenate(sel_i, axis=0).T
    jj_ref[...] = jnp.concatenate(sel_j, axis=0).T


def _route(q, keys, tm):
    t, dq = q.shape
    nslot = PEER_TOPK * PEER_HEADS
    out = jax.ShapeDtypeStruct((t, nslot), F32)
    oblk = pl.BlockSpec((tm, nslot), lambda i: (i, 0))
    return pl.pallas_call(
        _route_kernel,
        grid=(t // tm,),
        in_specs=[pl.BlockSpec((tm, dq), lambda i: (i, 0)),
                  pl.BlockSpec(keys.shape, lambda i: (0, 0, 0))],
        out_specs=[oblk, oblk, oblk],
        out_shape=[out, out, out],
        compiler_params=_params(1),
        name="peer_route",
    )(q, keys)


def _wbuild_kernel(g_ref, ii_ref, jj_ref, sv_ref, w_ref, buf_ref):
    tw = g_ref.shape[0]
    nk = PEER_NKEYS
    iota_s = lax.broadcasted_iota(jnp.int32, (nk, g_ref.shape[1]), 0).astype(F32)
    dn = (((1,), (1,)), ((), ()))

    def token_block(t):
        g = g_ref[pl.ds(t, 1), :]
        a = jnp.where(iota_s == ii_ref[pl.ds(t, 1), :], g, 0.0).astype(BF16)
        bt = jnp.where(iota_s == jj_ref[pl.ds(t, 1), :], 1.0, 0.0).astype(BF16)
        return lax.dot_general(a, bt, dn, preferred_element_type=F32) * sv_ref[...]

    def tok(t, carry):
        off = pl.multiple_of(t * W_PITCH, SUBLANES)
        buf_ref[pl.ds(off, nk), :] = token_block(t)
        return carry

    lax.fori_loop(0, tw, tok, 0, unroll=TOKEN_UNROLL)
    per = w_ref.shape[2] // nk
    for i in range(nk):
        w_ref[i // per, :, (i % per) * nk:(i % per + 1) * nk] = (
            buf_ref[pl.ds(i, tw, stride=W_PITCH), :].astype(BF16))


def _wbuild(gates, ii, jj, sv, tw, te):
    t, nslot = gates.shape
    ne = PEER_NKEYS * PEER_NKEYS
    blk = pl.BlockSpec((tw, nslot), lambda i: (i, 0))
    return pl.pallas_call(
        _wbuild_kernel,
        grid=(t // tw,),
        in_specs=[blk, blk, blk, pl.BlockSpec((PEER_NKEYS, PEER_NKEYS), lambda i: (0, 0))],
        out_specs=pl.BlockSpec((ne // te, tw, te), lambda i: (0, i, 0)),
        out_shape=jax.ShapeDtypeStruct((ne // te, t, te), BF16),
        scratch_shapes=[pltpu.VMEM((tw * W_PITCH, PEER_NKEYS), F32)],
        compiler_params=_params(1),
        name="peer_weights",
    )(gates, ii, jj, sv.reshape(PEER_NKEYS, PEER_NKEYS))


def _peer_step(h_ref, sh_ref, u_ref, su_ref, v_ref, w_ref, o_ref,
               act_new, act_old, p_new, s_new, p_old, s_old):
    tm = h_ref.shape[0]
    rc = tm // PEER_ROW_CHUNKS
    for c in range(PEER_ROW_CHUNKS):
        rows = slice(c * rc, (c + 1) * rc)
        a = act_old[rows, :] * sh_ref[rows, :] * su_ref[...]
        p = _gelu_tanh_bf16(a) * w_ref[rows, :]
        sp = _fp8_scale(jnp.max(jnp.abs(p), axis=-1, keepdims=True).astype(F32))
        p_new[rows, :] = (p.astype(F32) / sp).astype(FP8)
        s_new[rows, :] = sp
        act_new[rows, :] = lax.dot_general(h_ref[rows, :], u_ref[...], (((1,), (1,)), ((), ())),
                                           preferred_element_type=F32)
        o_ref[rows, :] += jnp.dot(p_old[rows, :], v_ref[...], preferred_element_type=F32) * s_old[rows, :]


def _peer_kernel(h_ref, sh_ref, u_ref, su_ref, v_ref, w_ref, x1_ref, gf_ref, o_ref,
                 act0_ref, act1_ref, p0_ref, s0_ref, p1_ref, s1_ref, *, ne, final_norm):
    g = pl.program_id(0)
    e_out = (g - 2) % ne

    @pl.when(g == 0)
    def _():
        act1_ref[...] = jnp.zeros(act1_ref.shape, F32)
        p0_ref[...] = jnp.zeros(p0_ref.shape, FP8)
        s0_ref[...] = jnp.ones(s0_ref.shape, F32)

    @pl.when((g < 2) | (e_out == 0))
    def _():
        o_ref[...] = jnp.zeros(o_ref.shape, F32)

    args = (h_ref, sh_ref, u_ref, su_ref, v_ref, w_ref, o_ref)

    @pl.when(g % 2 == 0)
    def _():
        _peer_step(*args, act0_ref, act1_ref, p1_ref, s1_ref, p0_ref, s0_ref)

    @pl.when(g % 2 == 1)
    def _():
        _peer_step(*args, act1_ref, act0_ref, p0_ref, s0_ref, p1_ref, s1_ref)

    @pl.when((g >= 2) & (e_out == ne - 1))
    def _():
        x = x1_ref[...] + o_ref[...]
        if final_norm:
            ms = jnp.mean(x * x, axis=-1, keepdims=True)
            x = x * lax.rsqrt(ms + NORM_EPS) * gf_ref[...]
        o_ref[...] = x


def _peer(h8, sh, u8, su, v8, w, x1, gf, final_norm, tm, te):
    t, d = h8.shape
    ne = u8.shape[0] // te
    nt = t // tm
    tok = lambda g: jnp.clip(g // ne, 0, nt - 1)
    return pl.pallas_call(
        functools.partial(_peer_kernel, ne=ne, final_norm=final_norm),
        grid=(nt * ne + 2,),
        in_specs=[pl.BlockSpec((tm, d), lambda g: (tok(g), 0)),
                  pl.BlockSpec((tm, 1), lambda g: (tok(g - 1), 0)),
                  pl.BlockSpec((te, d), lambda g: (g % ne, 0)),
                  pl.BlockSpec((1, te), lambda g: (0, (g - 1) % ne)),
                  pl.BlockSpec((te, d), lambda g: ((g - 2) % ne, 0)),
                  pl.BlockSpec((None, tm, te), lambda g: ((g - 1) % ne, tok(g - 1), 0)),
                  pl.BlockSpec((tm, d), lambda g: (tok(g - 2), 0), pipeline_mode=pl.Buffered(1)),
                  pl.BlockSpec((1, d), lambda g: (0, 0))],
        out_specs=pl.BlockSpec((tm, d), lambda g: (tok(g - 2), 0)),
        out_shape=jax.ShapeDtypeStruct((t, d), F32),
        scratch_shapes=[pltpu.VMEM((tm, te), F32), pltpu.VMEM((tm, te), F32),
                        pltpu.VMEM((tm, te), FP8), pltpu.VMEM((tm, 1), F32),
                        pltpu.VMEM((tm, te), FP8), pltpu.VMEM((tm, 1), F32)],
        compiler_params=pltpu.CompilerParams(dimension_semantics=("arbitrary",),
                                             vmem_limit_bytes=PEER_VMEM_LIMIT),
        name="peer_experts",
    )(h8, sh, u8, su, v8, w, x1, gf)


def _quantize_kernel(t_ref, q_ref, s_ref):
    x = t_ref[...]
    scale = _fp8_scale(jnp.max(jnp.abs(x), axis=1, keepdims=True))
    s_ref[...] = scale
    q_ref[...] = (x / scale).astype(FP8)


def _quantize_rows(tab, tr):
    n, d = tab.shape
    q, scale = pl.pallas_call(
        _quantize_kernel,
        grid=(n // tr,),
        in_specs=[pl.BlockSpec((tr, d), lambda i: (i, 0))],
        out_specs=[pl.BlockSpec((tr, d), lambda i: (i, 0)), pl.BlockSpec((tr, 1), lambda i: (i, 0))],
        out_shape=[jax.ShapeDtypeStruct((n, d), FP8), jax.ShapeDtypeStruct((n, 1), F32)],
        compiler_params=_params(1),
        name="quantize_rows",
    )(tab)
    return q, scale.reshape(1, n)


def _tiles(t, s):
    return dict(
        inproj_tm=min(1024, t), inproj_tn=512,
        seq_tt=min(256, s),
        scan_tt=min(32, s),
        outproj_tm=min(256, s),
        qproj_tm=min(512, t),
        route_tm=min(128, t),
        wbuild_tw=min(128, t),
        peer_tm=min(1024, t), peer_te=512,
        quant_tr=512,
    )


def _layer(x2, b, s, final_g, norm1_g, w_in, conv_w, conv_b, lru_wa, lru_ba, lru_wi, lru_bi, lru_lambda,
           rwkv_mu, rwkv_w0, rwkv_w2, rwkv_a0, rwkv_a2, rwkv_g2, rwkv_k_k, rwkv_k_a, rwkv_r_k,
           rwkv_lnx_w, rwkv_lnx_b, w_out, norm2_g, peer_wq, peer_keys, peer_u, peer_v):
    t, d = x2.shape
    c_lru = conv_w.shape[1]
    c_rwkv = rwkv_w0.shape[0]
    n_heads = c_rwkv // RWKV_HEAD_DIM
    tl = _tiles(t, s)
    row = lambda v: v.reshape(1, -1)

    n_main = 2 * c_lru + 3 * c_rwkv
    n_lora = DECAY_LORA + AAA_LORA + GATE_LORA
    w_in_p = jnp.pad(w_in, ((0, 0), (0, LORA_PAD - n_lora))).astype(BF16)
    proj = _inproj(x2, row(norm1_g), w_in_p, tl["inproj_tm"], tl["inproj_tn"])

    y_lru = _lru(proj, b, s, tl["seq_tt"], conv_w, row(conv_b), lru_wa.astype(BF16), row(lru_ba),
                 lru_wi.astype(BF16), row(lru_bi), row(lru_lambda))

    mu_main, mu_lora = rwkv_mu[:3 * c_rwkv], rwkv_mu[3 * c_rwkv:]
    mu_l = jnp.pad(mu_lora, (0, LORA_PAD - n_lora))
    w2p = jnp.pad(rwkv_w2, ((0, LANES - DECAY_LORA), (0, 0))).astype(BF16)
    a2p = jnp.pad(rwkv_a2, ((DECAY_LORA, LANES - DECAY_LORA - AAA_LORA), (0, 0))).astype(BF16)
    g2p = jnp.pad(rwkv_g2, ((0, 2 * LANES - GATE_LORA), (0, 0))).astype(BF16)
    seg = np.arange(LANES) // RWKV_HEAD_DIM
    e_blk = jnp.asarray(seg[:, None] == seg[None, :], dtype=BF16)
    assert b * n_heads == LANES, "the scan maps (batch, head) onto the 128 lanes"
    scan_in, g_s, bonus = _rwkv_prep(
        proj, b, s, tl["seq_tt"], c_rwkv, 2 * c_lru,
        row(mu_main[:c_rwkv]), row(mu_main[c_rwkv:2 * c_rwkv]), row(mu_main[2 * c_rwkv:]), row(mu_l),
        row(rwkv_w0), w2p, row(rwkv_a0), a2p, g2p, row(rwkv_k_k), row(rwkv_k_a),
        row(rwkv_r_k), e_blk)

    yn = _to_head_major(_rwkv_scan(_to_time_major(scan_in), tl["scan_tt"]))

    w_out_bf = w_out.astype(BF16)
    x1, h2, h8, sh = _outproj(x2, y_lru, yn, bonus, g_s, row(rwkv_lnx_w), row(rwkv_lnx_b),
                              w_out_bf[:c_lru], w_out_bf[c_lru:], row(norm2_g), tl["outproj_tm"])

    q = _qproj(h2, peer_wq.astype(BF16), tl["qproj_tm"])
    keys = peer_keys.reshape(2 * PEER_HEADS, PEER_NKEYS, -1)
    gates, ii, jj = _route(q, keys, tl["route_tm"])
    u8, su = _quantize_rows(peer_u, tl["quant_tr"])
    v8, sv = _quantize_rows(peer_v, tl["quant_tr"])
    w = _wbuild(gates, ii, jj, sv, tl["wbuild_tw"], tl["peer_te"])
    gf = jnp.ones((1, d), F32) if final_g is None else final_g.reshape(1, d)
    return _peer(h8, sh, u8, su, v8, w, x1, gf, final_g is not None, tl["peer_tm"], tl["peer_te"])


def kernel(x, norm1_g, w_in, conv_w, conv_b, lru_wa, lru_ba, lru_wi, lru_bi, lru_lambda, rwkv_mu, rwkv_w0, rwkv_w2, rwkv_a0, rwkv_a2, rwkv_g2, rwkv_k_k, rwkv_k_a, rwkv_r_k, rwkv_lnx_w, rwkv_lnx_b, w_out, norm2_g, peer_wq, peer_keys, peer_u, peer_v, normf_g):
    b, s, d = x.shape
    depth = norm1_g.shape[0]
    x2 = x.reshape(b * s, d)
    for l in range(depth):
        final_g = normf_g if l == depth - 1 else None
        x2 = _layer(x2, b, s, final_g, norm1_g[l], w_in[l], conv_w[l], conv_b[l], lru_wa[l], lru_ba[l],
                    lru_wi[l], lru_bi[l], lru_lambda[l], rwkv_mu[l], rwkv_w0[l], rwkv_w2[l],
                    rwkv_a0[l], rwkv_a2[l], rwkv_g2[l], rwkv_k_k[l], rwkv_k_a[l],
                    rwkv_r_k[l].reshape(-1), rwkv_lnx_w[l], rwkv_lnx_b[l], w_out[l], norm2_g[l],
                    peer_wq[l], peer_keys[l], peer_u[l], peer_v[l])
    return x2.reshape(b, s, d)
```

```python
import functools

import jax
import jax.numpy as jnp
import numpy as np
from jax import lax
from jax.experimental import pallas as pl
from jax.experimental.pallas import tpu as pltpu

NORM_EPS = 1e-6
GN_EPS = 64e-5
LRU_C = 8.0
LRU_HEADS = 8
CONV_WIDTH = 4
RWKV_HEAD_DIM = 64
DECAY_LORA = 64
AAA_LORA = 64
GATE_LORA = 160
PEER_HEADS = 8
PEER_NKEYS = 128
PEER_TOPK = 16

LANES = 128
SUBLANES = 8
LORA_PAD = 512
W_PITCH = 136
N_PITCH = 72
N_SCAN_IN = 6
PEER_ROW_CHUNKS = 2
TOKEN_UNROLL = 32
VMEM_LIMIT = 48 * 1024 * 1024
PEER_VMEM_LIMIT = 56 * 1024 * 1024

F32 = jnp.float32
BF16 = jnp.bfloat16
FP8 = jnp.float8_e4m3fn
FP8_TARGET = 256.0
TINY = 1e-30


def _fp8_scale(amax):
    return jnp.maximum(amax, TINY) * (1.0 / FP8_TARGET)


def _params(n_axes):
    return pltpu.CompilerParams(dimension_semantics=("arbitrary",) * n_axes,
                                vmem_limit_bytes=VMEM_LIMIT)


def _softplus(z):
    return jnp.maximum(z, 0.0) + jnp.log1p(jnp.exp(-jnp.abs(z)))


def _gelu_tanh_bf16(x):
    c = float(np.float32(np.sqrt(2.0 / np.pi)))
    k = float(np.float32(0.044715))
    arg = x * (c + (c * k) * (x * x))
    hx = x.astype(BF16) * 0.5
    return hx + hx * jnp.tanh(arg.astype(BF16))


def _split_bf16(x):
    hi = x.astype(BF16)
    lo = (x - hi.astype(F32)).astype(BF16)
    return hi, lo


def _segsum(x, e_blk):
    outs = []
    for c in range(x.shape[1] // LANES):
        hi, lo = _split_bf16(x[:, c * LANES:(c + 1) * LANES])
        outs.append(jnp.dot(hi, e_blk, preferred_element_type=F32)
                    + jnp.dot(lo, e_blk, preferred_element_type=F32))
    return jnp.concatenate(outs, axis=1)


def _inproj_kernel(x_ref, g_ref, w_ref, o_ref, h_ref):
    @pl.when(pl.program_id(1) == 0)
    def _():
        x = x_ref[...]
        ms = jnp.mean(x * x, axis=-1, keepdims=True)
        h_ref[...] = (x * lax.rsqrt(ms + NORM_EPS) * g_ref[...]).astype(BF16)

    o_ref[...] = jnp.dot(h_ref[...], w_ref[...], preferred_element_type=F32).astype(o_ref.dtype)


def _inproj(x2, g, w_bf, tm, tn):
    t, d = x2.shape
    n = w_bf.shape[1]
    return pl.pallas_call(
        _inproj_kernel,
        grid=(t // tm, n // tn),
        in_specs=[pl.BlockSpec((tm, d), lambda i, j: (i, 0)),
                  pl.BlockSpec((1, d), lambda i, j: (0, 0)),
                  pl.BlockSpec((d, tn), lambda i, j: (0, j))],
        out_specs=pl.BlockSpec((tm, tn), lambda i, j: (i, j)),
        out_shape=jax.ShapeDtypeStruct((t, n), BF16),
        scratch_shapes=[pltpu.VMEM((tm, d), BF16)],
        compiler_params=_params(2),
        name="inproj",
    )(x2, g, w_bf)


def _lru_kernel(xb_ref, gate_ref, cw_ref, cb_ref, wa_ref, ba_ref, wi_ref, bi_ref, lam_ref,
                o_ref, ext_ref, a_ref, b_ref, hcar_ref):
    tt, c = xb_ref.shape
    hd = c // LRU_HEADS

    @pl.when(pl.program_id(1) == 0)
    def _():
        ext_ref[0:SUBLANES, :] = jnp.zeros((SUBLANES, c), F32)
        hcar_ref[...] = jnp.zeros((1, c), F32)

    x = xb_ref[...].astype(F32)
    ext_ref[SUBLANES:SUBLANES + tt, :] = x
    xc = cb_ref[...] + cw_ref[CONV_WIDTH - 1:CONV_WIDTH, :] * x
    for k in range(1, CONV_WIDTH):
        xc = xc + cw_ref[CONV_WIDTH - 1 - k:CONV_WIDTH - k, :] * ext_ref[SUBLANES - k:SUBLANES - k + tt, :]
    ext_ref[0:SUBLANES, :] = x[tt - SUBLANES:tt, :]

    xcb = xc.astype(BF16)
    ra, ia = [], []
    for h in range(LRU_HEADS):
        xh = xcb[:, h * hd:(h + 1) * hd]
        ra.append(jnp.dot(xh, wa_ref[h], preferred_element_type=F32))
        ia.append(jnp.dot(xh, wi_ref[h], preferred_element_type=F32))
    r = jax.nn.sigmoid(jnp.concatenate(ra, axis=1) + ba_ref[...])
    ig = jax.nn.sigmoid(jnp.concatenate(ia, axis=1) + bi_ref[...])
    log_a = (-LRU_C) * r * _softplus(-lam_ref[...])
    a = jnp.exp(log_a)
    bx = jnp.sqrt(-jnp.tanh(log_a) * (a * a + 1.0)) * (ig * xc)

    row = lax.broadcasted_iota(jnp.int32, (tt, c), 0) % SUBLANES
    for d in (1, 2, 4):
        m = row >= d
        a_sh = jnp.where(m, pltpu.roll(a, d, 0), 1.0)
        b_sh = jnp.where(m, pltpu.roll(bx, d, 0), 0.0)
        bx = a * b_sh + bx
        a = a * a_sh
    a_ref[...] = a
    b_ref[...] = bx

    def blk(i, hc):
        off = pl.multiple_of(i * SUBLANES, SUBLANES)
        hb = a_ref[pl.ds(off, SUBLANES), :] * hc + b_ref[pl.ds(off, SUBLANES), :]
        b_ref[pl.ds(off, SUBLANES), :] = hb
        return hb[SUBLANES - 1:SUBLANES, :]

    hcar_ref[...] = lax.fori_loop(0, tt // SUBLANES, blk, hcar_ref[...])
    o_ref[...] = (b_ref[...] * jax.nn.gelu(gate_ref[...].astype(F32))).astype(BF16)


def _lru(proj, b, s, tt, cw, cb, wa, ba, wi, bi, lam):
    c = cw.shape[1]
    nt = s // tt
    row = lambda bb, ss: (bb * nt + ss, 0)
    const2 = lambda bb, ss: (0, 0)
    const3 = lambda bb, ss: (0, 0, 0)
    hd = c // LRU_HEADS
    return pl.pallas_call(
        _lru_kernel,
        grid=(b, nt),
        in_specs=[pl.BlockSpec((tt, c), row),
                  pl.BlockSpec((tt, c), lambda bb, ss: (bb * nt + ss, 1)),
                  pl.BlockSpec((CONV_WIDTH, c), const2),
                  pl.BlockSpec((1, c), const2),
                  pl.BlockSpec((LRU_HEADS, hd, hd), const3),
                  pl.BlockSpec((1, c), const2),
                  pl.BlockSpec((LRU_HEADS, hd, hd), const3),
                  pl.BlockSpec((1, c), const2),
                  pl.BlockSpec((1, c), const2)],
        out_specs=pl.BlockSpec((tt, c), row),
        out_shape=jax.ShapeDtypeStruct((b * s, c), BF16),
        scratch_shapes=[pltpu.VMEM((tt + SUBLANES, c), F32),
                        pltpu.VMEM((tt, c), F32),
                        pltpu.VMEM((tt, c), F32),
                        pltpu.VMEM((1, c), F32)],
        compiler_params=_params(2),
        name="rg_lru",
    )(proj, proj, cw, cb, wa, ba, wi, bi, lam)


def _token_shift(ext_ref, x, mu):
    tt = x.shape[0]
    ext_ref[SUBLANES:SUBLANES + tt, :] = x
    prev = ext_ref[SUBLANES - 1:SUBLANES - 1 + tt, :]
    ext_ref[0:SUBLANES, :] = x[tt - SUBLANES:tt, :]
    return x + (prev - x) * mu


def _store_head_major(o_ref, q, x):
    n = RWKV_HEAD_DIM
    per = LANES // n
    for cc in range(x.shape[1] // LANES):
        xt = x[:, cc * LANES:(cc + 1) * LANES].T
        for hh in range(per):
            h = per * cc + hh
            o_ref[q, h * N_PITCH:h * N_PITCH + n, :] = xt[hh * n:(hh + 1) * n, :]
            o_ref[q, h * N_PITCH + n:(h + 1) * N_PITCH, :] = jnp.zeros((N_PITCH - n, x.shape[0]), F32)


def _rwkv_prep_kernel(r_ref, k_ref, v_ref, l_ref, mur_ref, muk_ref, muv_ref, mul_ref,
                      w0_ref, w2_ref, a0_ref, a2_ref, g2_ref, kk_ref, ka_ref, rk_ref, e_ref,
                      so_ref, go_ref, bonus_ref,
                      er_ref, ek_ref, ev_ref, el_ref):
    @pl.when(pl.program_id(1) == 0)
    def _():
        for ext in (er_ref, ek_ref, ev_ref, el_ref):
            ext[0:SUBLANES, :] = jnp.zeros((SUBLANES, ext.shape[1]), F32)

    r = _token_shift(er_ref, r_ref[...].astype(F32), mur_ref[...])
    k = _token_shift(ek_ref, k_ref[...].astype(F32), muk_ref[...])
    v = _token_shift(ev_ref, v_ref[...].astype(F32), muv_ref[...])
    lo = _token_shift(el_ref, l_ref[...].astype(F32), mul_ref[...])

    lo_a = lo[:, 0:LANES]
    lo_g = lo[:, LANES:3 * LANES]
    wl = w0_ref[...] + jnp.dot(jnp.tanh(lo_a).astype(BF16), w2_ref[...], preferred_element_type=F32)
    w_log = -_softplus(-wl) - 0.5
    decay = jnp.exp(-jnp.exp(w_log))
    a = jax.nn.sigmoid(a0_ref[...] + jnp.dot(lo_a.astype(BF16), a2_ref[...], preferred_element_type=F32))
    g = jnp.dot(jax.nn.sigmoid(lo_g).astype(BF16), g2_ref[...], preferred_element_type=F32)

    e_blk = e_ref[...]
    kk = k * kk_ref[...]
    kk = kk / jnp.maximum(jnp.sqrt(_segsum(kk * kk, e_blk)), 1e-12)
    kmod = k * (1.0 + (a - 1.0) * ka_ref[...])

    for q, val in enumerate((r, decay, kmod, v, -kk, kk * a)):
        _store_head_major(so_ref, q, val)
    go_ref[...] = g
    bonus_ref[...] = _segsum(r * kmod * rk_ref[...], e_blk) * v


def _rwkv_prep(proj, b, s, tt, c, col0, mur, muk, muv, mul, w0, w2p, a0, a2p, g2p, kk, ka, rk, e_blk):
    nt = s // tt
    cb0 = col0 // c
    lb = (col0 + 3 * c) // LORA_PAD
    const2 = lambda bb, ss: (0, 0)
    vec = pl.BlockSpec((1, c), const2)
    row = lambda bb, ss: (bb * nt + ss, 0)
    out = jax.ShapeDtypeStruct((b * s, c), F32)
    hrows = (c // RWKV_HEAD_DIM) * N_PITCH
    return pl.pallas_call(
        _rwkv_prep_kernel,
        grid=(b, nt),
        in_specs=[pl.BlockSpec((tt, c), lambda bb, ss: (bb * nt + ss, cb0)),
                  pl.BlockSpec((tt, c), lambda bb, ss: (bb * nt + ss, cb0 + 1)),
                  pl.BlockSpec((tt, c), lambda bb, ss: (bb * nt + ss, cb0 + 2)),
                  pl.BlockSpec((tt, LORA_PAD), lambda bb, ss: (bb * nt + ss, lb)),
                  vec, vec, vec, pl.BlockSpec((1, LORA_PAD), const2),
                  vec, pl.BlockSpec((LANES, c), const2),
                  vec, pl.BlockSpec((LANES, c), const2),
                  pl.BlockSpec((2 * LANES, c), const2),
                  vec, vec, vec, pl.BlockSpec((LANES, LANES), const2)],
        out_specs=[pl.BlockSpec((N_SCAN_IN, hrows, tt), lambda bb, ss: (0, bb, ss)),
                   pl.BlockSpec((tt, c), row), pl.BlockSpec((tt, c), row)],
        out_shape=[jax.ShapeDtypeStruct((N_SCAN_IN, b * hrows, s), F32), out, out],
        scratch_shapes=[pltpu.VMEM((tt + SUBLANES, c), F32),
                        pltpu.VMEM((tt + SUBLANES, c), F32),
                        pltpu.VMEM((tt + SUBLANES, c), F32),
                        pltpu.VMEM((tt + SUBLANES, LORA_PAD), F32)],
        compiler_params=_params(2),
        name="rwkv_prep",
    )(proj, proj, proj, proj, mur, muk, muv, mul, w0, w2p, a0, a2p, g2p, kk, ka, rk, e_blk)


def _swap_kernel(i_ref, o_ref):
    for n in range(RWKV_HEAD_DIM):
        o_ref[pl.ds(n, LANES, stride=N_PITCH), :] = i_ref[pl.ds(n, LANES, stride=N_PITCH), :].T
    for n in range(RWKV_HEAD_DIM, N_PITCH):
        o_ref[pl.ds(n, LANES, stride=N_PITCH), :] = jnp.zeros((LANES, LANES), F32)


def _to_time_major(p):
    nq, rows, s = p.shape
    return pl.pallas_call(
        _swap_kernel,
        grid=(nq, s // LANES),
        in_specs=[pl.BlockSpec((None, rows, LANES), lambda q, i: (q, 0, i))],
        out_specs=pl.BlockSpec((None, rows, LANES), lambda q, i: (q, i, 0)),
        out_shape=jax.ShapeDtypeStruct((nq, s * N_PITCH, LANES), F32),
        compiler_params=_params(2),
        name="to_time_major",
    )(p)


def _to_head_major(y):
    rows_total, bh = y.shape
    s = rows_total // N_PITCH
    rows = LANES * N_PITCH
    return pl.pallas_call(
        _swap_kernel,
        grid=(s // LANES,),
        in_specs=[pl.BlockSpec((rows, LANES), lambda i: (i, 0))],
        out_specs=pl.BlockSpec((rows, LANES), lambda i: (0, i)),
        out_shape=jax.ShapeDtypeStruct((bh * N_PITCH, s), F32),
        compiler_params=_params(1),
        name="to_head_major",
    )(y)


def _rwkv_scan_kernel(q_ref, y_ref, s_ref, sa_ref):
    n = RWKV_HEAD_DIM
    tt = y_ref.shape[0] // N_PITCH

    @pl.when(pl.program_id(0) == 0)
    def _():
        s_ref[...] = jnp.zeros(s_ref.shape, F32)

    def step(t, carry):
        base = pl.multiple_of(t * N_PITCH, SUBLANES)
        r = q_ref[0, pl.ds(base, n), :]
        w = q_ref[1, pl.ds(base, n), :]
        k = q_ref[2, pl.ds(base, n), :]
        na = q_ref[4, pl.ds(base, n), :]
        b = q_ref[5, pl.ds(base, n), :]
        for i in range(n):
            sa_ref[i:i + 1, :] = jnp.sum(s_ref[i] * na, axis=0, keepdims=True)
        for i in range(n):
            s_new = s_ref[i] * w + sa_ref[i:i + 1, :] * b + q_ref[3, pl.ds(base + i, 1), :] * k
            s_ref[i] = s_new
            y_ref[pl.ds(base + i, 1), :] = jnp.sum(s_new * r, axis=0, keepdims=True)
        y = y_ref[pl.ds(base, n), :]
        mean = jnp.mean(y, axis=0, keepdims=True)
        yc = y - mean
        var = jnp.mean(yc * yc, axis=0, keepdims=True)
        y_ref[pl.ds(base, n), :] = yc * lax.rsqrt(var + GN_EPS)
        y_ref[pl.ds(base + n, N_PITCH - n), :] = jnp.zeros((N_PITCH - n, y_ref.shape[1]), F32)
        return carry

    lax.fori_loop(0, tt, step, 0)


def _rwkv_scan(q, tt):
    nq, rows_total, bh = q.shape
    s = rows_total // N_PITCH
    n = RWKV_HEAD_DIM
    return pl.pallas_call(
        _rwkv_scan_kernel,
        grid=(s // tt,),
        in_specs=[pl.BlockSpec((nq, tt * N_PITCH, bh), lambda i: (0, i, 0))],
        out_specs=pl.BlockSpec((tt * N_PITCH, bh), lambda i: (i, 0)),
        out_shape=jax.ShapeDtypeStruct((rows_total, bh), F32),
        scratch_shapes=[pltpu.VMEM((n, n, bh), F32), pltpu.VMEM((n, bh), F32)],
        compiler_params=_params(1),
        name="rwkv_scan",
    )(q)


def _outproj_kernel(x_ref, ylru_ref, yn_ref, bonus_ref, g_ref, lw_ref, lb_ref, wo1_ref, wo2_ref,
                    n2_ref, x1_ref, h2_ref, h8_ref, sh_ref):
    n = RWKV_HEAD_DIM
    per = LANES // n
    cols = []
    for cc in range(bonus_ref.shape[1] // LANES):
        blk = jnp.concatenate([yn_ref[(per * cc + hh) * N_PITCH:(per * cc + hh) * N_PITCH + n, :]
                               for hh in range(per)], axis=0)
        cols.append(blk.T)
    yn = jnp.concatenate(cols, axis=1)
    y_rwkv = ((yn * lw_ref[...] + lb_ref[...] + bonus_ref[...]) * g_ref[...]).astype(BF16)
    x1 = (x_ref[...]
          + jnp.dot(ylru_ref[...], wo1_ref[...], preferred_element_type=F32)
          + jnp.dot(y_rwkv, wo2_ref[...], preferred_element_type=F32))
    x1_ref[...] = x1
    ms = jnp.mean(x1 * x1, axis=-1, keepdims=True)
    h2 = x1 * lax.rsqrt(ms + NORM_EPS) * n2_ref[...]
    h2_ref[...] = h2.astype(BF16)
    sh = _fp8_scale(jnp.max(jnp.abs(h2), axis=-1, keepdims=True))
    sh_ref[...] = sh
    h8_ref[...] = (h2 / sh).astype(FP8)


def _outproj(x2, ylru, yn, bonus, g, lw, lb, wo1, wo2, n2, tm):
    t, d = x2.shape
    c = ylru.shape[1]
    s = yn.shape[1]
    nst = s // tm
    hrows = (c // RWKV_HEAD_DIM) * N_PITCH
    row_d = pl.BlockSpec((tm, d), lambda i: (i, 0))
    row_c = pl.BlockSpec((tm, c), lambda i: (i, 0))
    const = lambda i: (0, 0)
    return pl.pallas_call(
        _outproj_kernel,
        grid=(t // tm,),
        in_specs=[row_d, row_c, pl.BlockSpec((hrows, tm), lambda i: (i // nst, i % nst)), row_c, row_c,
                  pl.BlockSpec((1, c), const), pl.BlockSpec((1, c), const),
                  pl.BlockSpec((c, d), const), pl.BlockSpec((c, d), const),
                  pl.BlockSpec((1, d), const)],
        out_specs=[row_d, row_d, row_d, pl.BlockSpec((tm, 1), lambda i: (i, 0))],
        out_shape=[jax.ShapeDtypeStruct((t, d), F32), jax.ShapeDtypeStruct((t, d), BF16),
                   jax.ShapeDtypeStruct((t, d), FP8), jax.ShapeDtypeStruct((t, 1), F32)],
        compiler_params=_params(1),
        name="outproj",
    )(x2, ylru, yn, bonus, g, lw, lb, wo1, wo2, n2)


def _matmul_kernel(a_ref, w_ref, o_ref):
    o_ref[...] = jnp.dot(a_ref[...], w_ref[...], preferred_element_type=F32)


def _qproj(h2, wq, tm):
    t, d = h2.shape
    n = wq.shape[1]
    return pl.pallas_call(
        _matmul_kernel,
        grid=(t // tm,),
        in_specs=[pl.BlockSpec((tm, d), lambda i: (i, 0)),
                  pl.BlockSpec((d, n), lambda i: (0, 0))],
        out_specs=pl.BlockSpec((tm, n), lambda i: (i, 0)),
        out_shape=jax.ShapeDtypeStruct((t, n), F32),
        compiler_params=_params(1),
        name="qproj",
    )(h2, wq)


def _dot_nt3(a, b):
    dn = (((1,), (1,)), ((), ()))
    ah, al = _split_bf16(a)
    bh, bl = _split_bf16(b)
    return (lax.dot_general(ah, bh, dn, preferred_element_type=F32)
            + lax.dot_general(ah, bl, dn, preferred_element_type=F32)
            + lax.dot_general(al, bh, dn, preferred_element_type=F32))


def _tree(op, xs):
    xs = list(xs)
    while len(xs) > 1:
        nxt = [op(xs[i], xs[i + 1]) for i in range(0, len(xs) - 1, 2)]
        if len(xs) % 2:
            nxt.append(xs[-1])
        xs = nxt
    return xs[0]


def _route_kernel(q_ref, keys_ref, g_ref, ii_ref, jj_ref):
    tm = q_ref.shape[0]
    nk = PEER_NKEYS
    neg = -jnp.inf
    iota_n = lax.broadcasted_iota(jnp.int32, (nk, tm), 0).astype(F32)

    vals = [[None, None] for _ in range(PEER_HEADS)]
    idxs = [[None, None] for _ in range(PEER_HEADS)]
    for h in range(PEER_HEADS):
        for p in range(2):
            hp = 2 * h + p
            cur = _dot_nt3(keys_ref[hp], q_ref[:, hp * nk:(hp + 1) * nk])
            vs, ix = [], []
            for _ in range(PEER_TOPK):
                m = jnp.max(cur, axis=0, keepdims=True)
                idx = jnp.min(jnp.where(cur == m, iota_n, float(nk)), axis=0, keepdims=True)
                vs.append(m)
                ix.append(idx)
                cur = jnp.where(iota_n == idx, neg, cur)
            vals[h][p] = vs
            idxs[h][p] = ix

    def by_rank(table, p, rnk):
        return jnp.concatenate([table[h][p][rnk] for h in range(PEER_HEADS)], axis=0)

    va = [by_rank(vals, 0, r) for r in range(PEER_TOPK)]
    vb = [by_rank(vals, 1, r) for r in range(PEER_TOPK)]
    ia = [by_rank(idxs, 0, r) for r in range(PEER_TOPK)]
    ib = [by_rank(idxs, 1, r) for r in range(PEER_TOPK)]

    cand = []
    for p in range(PEER_TOPK):
        for qq in range(PEER_TOPK // (p + 1)):
            cand.append([va[p] + vb[qq], ia[p], ib[qq]])
    ncand = len(cand)

    best, sel_i, sel_j = [], [], []
    for _ in range(PEER_TOPK):
        m = _tree(jnp.maximum, [c[0] for c in cand])
        code = _tree(jnp.minimum, [jnp.where(c[0] == m, float(ci), float(ncand))
                                   for ci, c in enumerate(cand)])
        ci_sel = jnp.zeros_like(m)
        cj_sel = jnp.zeros_like(m)
        for ci, c in enumerate(cand):
            win = code == float(ci)
            ci_sel = jnp.where(win, c[1], ci_sel)
            cj_sel = jnp.where(win, c[2], cj_sel)
            c[0] = jnp.where(win, neg, c[0])
        best.append(m)
        sel_i.append(ci_sel)
        sel_j.append(cj_sel)

    ex = [jnp.exp(bv - best[0]) for bv in best]
    z = _tree(jnp.add, ex)
    gates = jnp.concatenate([e / z for e in ex], axis=0)
    g_ref[...] = gates.T
    ii_ref[...] = jnp.concatenate(sel_i, axis=0).T
    jj_ref[...] = jnp.concatenate(sel_j, axis=0).T


def _route(q, keys, tm):
    t, dq = q.shape
    nslot = PEER_TOPK * PEER_HEADS
    out = jax.ShapeDtypeStruct((t, nslot), F32)
    oblk = pl.BlockSpec((tm, nslot), lambda i: (i, 0))
    return pl.pallas_call(
        _route_kernel,
        grid=(t // tm,),
        in_specs=[pl.BlockSpec((tm, dq), lambda i: (i, 0)),
                  pl.BlockSpec(keys.shape, lambda i: (0, 0, 0))],
        out_specs=[oblk, oblk, oblk],
        out_shape=[out, out, out],
        compiler_params=_params(1),
        name="peer_route",
    )(q, keys)


def _wbuild_kernel(g_ref, ii_ref, jj_ref, sv_ref, w_ref, buf_ref):
    tw = g_ref.shape[0]
    nk = PEER_NKEYS
    iota_s = lax.broadcasted_iota(jnp.int32, (nk, g_ref.shape[1]), 0).astype(F32)
    dn = (((1,), (1,)), ((), ()))

    def token_block(t):
        g = g_ref[pl.ds(t, 1), :]
        a = jnp.where(iota_s == ii_ref[pl.ds(t, 1), :], g, 0.0).astype(BF16)
        bt = jnp.where(iota_s == jj_ref[pl.ds(t, 1), :], 1.0, 0.0).astype(BF16)
        return lax.dot_general(a, bt, dn, preferred_element_type=F32) * sv_ref[...]

    def tok(t, carry):
        off = pl.multiple_of(t * W_PITCH, SUBLANES)
        buf_ref[pl.ds(off, nk), :] = token_block(t)
        return carry

    lax.fori_loop(0, tw, tok, 0, unroll=TOKEN_UNROLL)
    per = w_ref.shape[2] // nk
    for i in range(nk):
        w_ref[i // per, :, (i % per) * nk:(i % per + 1) * nk] = (
            buf_ref[pl.ds(i, tw, stride=W_PITCH), :].astype(BF16))


def _wbuild(gates, ii, jj, sv, tw, te):
    t, nslot = gates.shape
    ne = PEER_NKEYS * PEER_NKEYS
    blk = pl.BlockSpec((tw, nslot), lambda i: (i, 0))
    return pl.pallas_call(
        _wbuild_kernel,
        grid=(t // tw,),
        in_specs=[blk, blk, blk, pl.BlockSpec((PEER_NKEYS, PEER_NKEYS), lambda i: (0, 0))],
        out_specs=pl.BlockSpec((ne // te, tw, te), lambda i: (0, i, 0)),
        out_shape=jax.ShapeDtypeStruct((ne // te, t, te), BF16),
        scratch_shapes=[pltpu.VMEM((tw * W_PITCH, PEER_NKEYS), F32)],
        compiler_params=_params(1),
        name="peer_weights",
    )(gates, ii, jj, sv.reshape(PEER_NKEYS, PEER_NKEYS))


def _peer_step(h_ref, sh_ref, u_ref, su_ref, v_ref, w_ref, o_ref,
               act_new, act_old, p_new, s_new, p_old, s_old):
    tm = h_ref.shape[0]
    rc = tm // PEER_ROW_CHUNKS
    for c in range(PEER_ROW_CHUNKS):
        rows = slice(c * rc, (c + 1) * rc)
        a = act_old[rows, :] * sh_ref[rows, :] * su_ref[...]
        p = _gelu_tanh_bf16(a) * w_ref[rows, :]
        sp = _fp8_scale(jnp.max(jnp.abs(p), axis=-1, keepdims=True).astype(F32))
        p_new[rows, :] = (p.astype(F32) / sp).astype(FP8)
        s_new[rows, :] = sp
        act_new[rows, :] = lax.dot_general(h_ref[rows, :], u_ref[...], (((1,), (1,)), ((), ())),
                                           preferred_element_type=F32)
        o_ref[rows, :] += jnp.dot(p_old[rows, :], v_ref[...], preferred_element_type=F32) * s_old[rows, :]


def _peer_kernel(h_ref, sh_ref, u_ref, su_ref, v_ref, w_ref, x1_ref, gf_ref, o_ref,
                 act0_ref, act1_ref, p0_ref, s0_ref, p1_ref, s1_ref, *, ne, final_norm):
    g = pl.program_id(0)
    e_out = (g - 2) % ne

    @pl.when(g == 0)
    def _():
        act1_ref[...] = jnp.zeros(act1_ref.shape, F32)
        p0_ref[...] = jnp.zeros(p0_ref.shape, FP8)
        s0_ref[...] = jnp.ones(s0_ref.shape, F32)

    @pl.when((g < 2) | (e_out == 0))
    def _():
        o_ref[...] = jnp.zeros(o_ref.shape, F32)

    args = (h_ref, sh_ref, u_ref, su_ref, v_ref, w_ref, o_ref)

    @pl.when(g % 2 == 0)
    def _():
        _peer_step(*args, act0_ref, act1_ref, p1_ref, s1_ref, p0_ref, s0_ref)

    @pl.when(g % 2 == 1)
    def _():
        _peer_step(*args, act1_ref, act0_ref, p0_ref, s0_ref, p1_ref, s1_ref)

    @pl.when((g >= 2) & (e_out == ne - 1))
    def _():
        x = x1_ref[...] + o_ref[...]
        if final_norm:
            ms = jnp.mean(x * x, axis=-1, keepdims=True)
            x = x * lax.rsqrt(ms + NORM_EPS) * gf_ref[...]
        o_ref[...] = x


def _peer(h8, sh, u8, su, v8, w, x1, gf, final_norm, tm, te):
    t, d = h8.shape
    ne = u8.shape[0] // te
    nt = t // tm
    tok = lambda g: jnp.clip(g // ne, 0, nt - 1)
    return pl.pallas_call(
        functools.partial(_peer_kernel, ne=ne, final_norm=final_norm),
        grid=(nt * ne + 2,),
        in_specs=[pl.BlockSpec((tm, d), lambda g: (tok(g), 0)),
                  pl.BlockSpec((tm, 1), lambda g: (tok(g - 1), 0)),
                  pl.BlockSpec((te, d), lambda g: (g % ne, 0)),
                  pl.BlockSpec((1, te), lambda g: (0, (g - 1) % ne)),
                  pl.BlockSpec((te, d), lambda g: ((g - 2) % ne, 0)),
                  pl.BlockSpec((None, tm, te), lambda g: ((g - 1) % ne, tok(g - 1), 0)),
                  pl.BlockSpec((tm, d), lambda g: (tok(g - 2), 0), pipeline_mode=pl.Buffered(1)),
                  pl.BlockSpec((1, d), lambda g: (0, 0))],
        out_specs=pl.BlockSpec((tm, d), lambda g: (tok(g - 2), 0)),
        out_shape=jax.ShapeDtypeStruct((t, d), F32),
        scratch_shapes=[pltpu.VMEM((tm, te), F32), pltpu.VMEM((tm, te), F32),
                        pltpu.VMEM((tm, te), FP8), pltpu.VMEM((tm, 1), F32),
                        pltpu.VMEM((tm, te), FP8), pltpu.VMEM((tm, 1), F32)],
        compiler_params=pltpu.CompilerParams(dimension_semantics=("arbitrary",),
                                             vmem_limit_bytes=PEER_VMEM_LIMIT),
        name="peer_experts",
    )(h8, sh, u8, su, v8, w, x1, gf)


def _quantize_kernel(t_ref, q_ref, s_ref):
    x = t_ref[...]
    scale = _fp8_scale(jnp.max(jnp.abs(x), axis=1, keepdims=True))
    s_ref[...] = scale
    q_ref[...] = (x / scale).astype(FP8)


def _quantize_rows(tab, tr):
    n, d = tab.shape
    q, scale = pl.pallas_call(
        _quantize_kernel,
        grid=(n // tr,),
        in_specs=[pl.BlockSpec((tr, d), lambda i: (i, 0))],
        out_specs=[pl.BlockSpec((tr, d), lambda i: (i, 0)), pl.BlockSpec((tr, 1), lambda i: (i, 0))],
        out_shape=[jax.ShapeDtypeStruct((n, d), FP8), jax.ShapeDtypeStruct((n, 1), F32)],
        compiler_params=_params(1),
        name="quantize_rows",
    )(tab)
    return q, scale.reshape(1, n)


def _tiles(t, s):
    return dict(
        inproj_tm=min(1024, t), inproj_tn=512,
        seq_tt=min(256, s),
        scan_tt=min(32, s),
        outproj_tm=min(256, s),
        qproj_tm=min(512, t),
        route_tm=min(256, t),
        wbuild_tw=min(128, t),
        peer_tm=min(1024, t), peer_te=512,
        quant_tr=512,
    )


def _layer(x2, b, s, final_g, norm1_g, w_in, conv_w, conv_b, lru_wa, lru_ba, lru_wi, lru_bi, lru_lambda,
           rwkv_mu, rwkv_w0, rwkv_w2, rwkv_a0, rwkv_a2, rwkv_g2, rwkv_k_k, rwkv_k_a, rwkv_r_k,
           rwkv_lnx_w, rwkv_lnx_b, w_out, norm2_g, peer_wq, peer_keys, peer_u, peer_v):
    t, d = x2.shape
    c_lru = conv_w.shape[1]
    c_rwkv = rwkv_w0.shape[0]
    n_heads = c_rwkv // RWKV_HEAD_DIM
    tl = _tiles(t, s)
    row = lambda v: v.reshape(1, -1)

    n_lora = DECAY_LORA + AAA_LORA + GATE_LORA
    w_in_p = jnp.pad(w_in, ((0, 0), (0, LORA_PAD - n_lora))).astype(BF16)
    proj = _inproj(x2, row(norm1_g), w_in_p, tl["inproj_tm"], tl["inproj_tn"])

    y_lru = _lru(proj, b, s, tl["seq_tt"], conv_w, row(conv_b), lru_wa.astype(BF16), row(lru_ba),
                 lru_wi.astype(BF16), row(lru_bi), row(lru_lambda))

    mu_main, mu_lora = rwkv_mu[:3 * c_rwkv], rwkv_mu[3 * c_rwkv:]
    mu_l = jnp.pad(mu_lora, (0, LORA_PAD - n_lora))
    w2p = jnp.pad(rwkv_w2, ((0, LANES - DECAY_LORA), (0, 0))).astype(BF16)
    a2p = jnp.pad(rwkv_a2, ((DECAY_LORA, LANES - DECAY_LORA - AAA_LORA), (0, 0))).astype(BF16)
    g2p = jnp.pad(rwkv_g2, ((0, 2 * LANES - GATE_LORA), (0, 0))).astype(BF16)
    seg = np.arange(LANES) // RWKV_HEAD_DIM
    e_blk = jnp.asarray(seg[:, None] == seg[None, :], dtype=BF16)
    assert b * n_heads == LANES, "the scan maps (batch, head) onto the 128 lanes"
    scan_in, g_s, bonus = _rwkv_prep(
        proj, b, s, tl["seq_tt"], c_rwkv, 2 * c_lru,
        row(mu_main[:c_rwkv]), row(mu_main[c_rwkv:2 * c_rwkv]), row(mu_main[2 * c_rwkv:]), row(mu_l),
        row(rwkv_w0), w2p, row(rwkv_a0), a2p, g2p, row(rwkv_k_k), row(rwkv_k_a),
        row(rwkv_r_k), e_blk)

    yn = _to_head_major(_rwkv_scan(_to_time_major(scan_in), tl["scan_tt"]))

    w_out_bf = w_out.astype(BF16)
    x1, h2, h8, sh = _outproj(x2, y_lru, yn, bonus, g_s, row(rwkv_lnx_w), row(rwkv_lnx_b),
                              w_out_bf[:c_lru], w_out_bf[c_lru:], row(norm2_g), tl["outproj_tm"])

    q = _qproj(h2, peer_wq.astype(BF16), tl["qproj_tm"])
    keys = peer_keys.reshape(2 * PEER_HEADS, PEER_NKEYS, -1)
    gates, ii, jj = _route(q, keys, tl["route_tm"])
    u8, su = _quantize_rows(peer_u, tl["quant_tr"])
    v8, sv = _quantize_rows(peer_v, tl["quant_tr"])
    w = _wbuild(gates, ii, jj, sv, tl["wbuild_tw"], tl["peer_te"])
    gf = jnp.ones((1, d), F32) if final_g is None else final_g.reshape(1, d)
    return _peer(h8, sh, u8, su, v8, w, x1, gf, final_g is not None, tl["peer_tm"], tl["peer_te"])


def kernel(x, norm1_g, w_in, conv_w, conv_b, lru_wa, lru_ba, lru_wi, lru_bi, lru_lambda, rwkv_mu, rwkv_w0, rwkv_w2, rwkv_a0, rwkv_a2, rwkv_g2, rwkv_k_k, rwkv_k_a, rwkv_r_k, rwkv_lnx_w, rwkv_lnx_b, w_out, norm2_g, peer_wq, peer_keys, peer_u, peer_v, normf_g):
    b, s, d = x.shape
    depth = norm1_g.shape[0]
    x2 = x.reshape(b * s, d)
    for l in range(depth):
        final_g = normf_g if l == depth - 1 else None
        x2 = _layer(x2, b, s, final_g, norm1_g[l], w_in[l], conv_w[l], conv_b[l], lru_wa[l], lru_ba[l],
                    lru_wi[l], lru_bi[l], lru_lambda[l], rwkv_mu[l], rwkv_w0[l], rwkv_w2[l],
                    rwkv_a0[l], rwkv_a2[l], rwkv_g2[l], rwkv_k_k[l], rwkv_k_a[l],
                    rwkv_r_k[l].reshape(-1), rwkv_lnx_w[l], rwkv_lnx_b[l], w_out[l], norm2_g[l],
                    peer_wq[l], peer_keys[l], peer_u[l], peer_v[l])
    return x2.reshape(b, s, d)
```

```python
import functools

import jax
import jax.numpy as jnp
import numpy as np
from jax import lax
from jax.experimental import pallas as pl
from jax.experimental.pallas import tpu as pltpu

NORM_EPS = 1e-6
GN_EPS = 64e-5
LRU_C = 8.0
LRU_HEADS = 8
CONV_WIDTH = 4
RWKV_HEAD_DIM = 64
DECAY_LORA = 64
AAA_LORA = 64
GATE_LORA = 160
PEER_HEADS = 8
PEER_NKEYS = 128
PEER_TOPK = 16

LANES = 128
SUBLANES = 8
LORA_PAD = 512
W_PITCH = 136
N_PITCH = 72
N_SCAN_IN = 6
PEER_ROW_CHUNKS = 2
TOKEN_UNROLL = 32
VMEM_LIMIT = 48 * 1024 * 1024
PEER_VMEM_LIMIT = 56 * 1024 * 1024

F32 = jnp.float32
BF16 = jnp.bfloat16
FP8 = jnp.float8_e4m3fn
FP8_TARGET = 256.0
TINY = 1e-30


def _fp8_scale(amax):
    return jnp.maximum(amax, TINY) * (1.0 / FP8_TARGET)


def _params(n_axes):
    return pltpu.CompilerParams(dimension_semantics=("arbitrary",) * n_axes,
                                vmem_limit_bytes=VMEM_LIMIT)


def _softplus(z):
    return jnp.maximum(z, 0.0) + jnp.log1p(jnp.exp(-jnp.abs(z)))


def _gelu_tanh_bf16(x):
    c = float(np.float32(np.sqrt(2.0 / np.pi)))
    k = float(np.float32(0.044715))
    arg = x * (c + (c * k) * (x * x))
    hx = x.astype(BF16) * 0.5
    return hx + hx * jnp.tanh(arg.astype(BF16))


def _split_bf16(x):
    hi = x.astype(BF16)
    lo = (x - hi.astype(F32)).astype(BF16)
    return hi, lo


def _segsum(x, e_blk):
    outs = []
    for c in range(x.shape[1] // LANES):
        hi, lo = _split_bf16(x[:, c * LANES:(c + 1) * LANES])
        outs.append(jnp.dot(hi, e_blk, preferred_element_type=F32)
                    + jnp.dot(lo, e_blk, preferred_element_type=F32))
    return jnp.concatenate(outs, axis=1)


def _inproj_kernel(x_ref, g_ref, w_ref, o_ref, h_ref):
    @pl.when(pl.program_id(1) == 0)
    def _():
        x = x_ref[...]
        ms = jnp.mean(x * x, axis=-1, keepdims=True)
        h_ref[...] = (x * lax.rsqrt(ms + NORM_EPS) * g_ref[...]).astype(BF16)

    o_ref[...] = jnp.dot(h_ref[...], w_ref[...], preferred_element_type=F32).astype(o_ref.dtype)


def _inproj(x2, g, w_bf, tm, tn):
    t, d = x2.shape
    n = w_bf.shape[1]
    return pl.pallas_call(
        _inproj_kernel,
        grid=(t // tm, n // tn),
        in_specs=[pl.BlockSpec((tm, d), lambda i, j: (i, 0)),
                  pl.BlockSpec((1, d), lambda i, j: (0, 0)),
                  pl.BlockSpec((d, tn), lambda i, j: (0, j))],
        out_specs=pl.BlockSpec((tm, tn), lambda i, j: (i, j)),
        out_shape=jax.ShapeDtypeStruct((t, n), BF16),
        scratch_shapes=[pltpu.VMEM((tm, d), BF16)],
        compiler_params=_params(2),
        name="inproj",
    )(x2, g, w_bf)


def _lru_kernel(xb_ref, gate_ref, cw_ref, cb_ref, wa_ref, ba_ref, wi_ref, bi_ref, lam_ref,
                o_ref, ext_ref, a_ref, b_ref, hcar_ref):
    tt, c = xb_ref.shape
    hd = c // LRU_HEADS

    @pl.when(pl.program_id(1) == 0)
    def _():
        ext_ref[0:SUBLANES, :] = jnp.zeros((SUBLANES, c), F32)
        hcar_ref[...] = jnp.zeros((1, c), F32)

    x = xb_ref[...].astype(F32)
    ext_ref[SUBLANES:SUBLANES + tt, :] = x
    xc = cb_ref[...] + cw_ref[CONV_WIDTH - 1:CONV_WIDTH, :] * x
    for k in range(1, CONV_WIDTH):
        xc = xc + cw_ref[CONV_WIDTH - 1 - k:CONV_WIDTH - k, :] * ext_ref[SUBLANES - k:SUBLANES - k + tt, :]
    ext_ref[0:SUBLANES, :] = x[tt - SUBLANES:tt, :]

    xcb = xc.astype(BF16)
    ra, ia = [], []
    for h in range(LRU_HEADS):
        xh = xcb[:, h * hd:(h + 1) * hd]
        ra.append(jnp.dot(xh, wa_ref[h], preferred_element_type=F32))
        ia.append(jnp.dot(xh, wi_ref[h], preferred_element_type=F32))
    r = jax.nn.sigmoid(jnp.concatenate(ra, axis=1) + ba_ref[...])
    ig = jax.nn.sigmoid(jnp.concatenate(ia, axis=1) + bi_ref[...])
    log_a = (-LRU_C) * r * _softplus(-lam_ref[...])
    a = jnp.exp(log_a)
    bx = jnp.sqrt(-jnp.tanh(log_a) * (a * a + 1.0)) * (ig * xc)

    row = lax.broadcasted_iota(jnp.int32, (tt, c), 0) % SUBLANES
    for d in (1, 2, 4):
        m = row >= d
        a_sh = jnp.where(m, pltpu.roll(a, d, 0), 1.0)
        b_sh = jnp.where(m, pltpu.roll(bx, d, 0), 0.0)
        bx = a * b_sh + bx
        a = a * a_sh
    a_ref[...] = a
    b_ref[...] = bx

    def blk(i, hc):
        off = pl.multiple_of(i * SUBLANES, SUBLANES)
        hb = a_ref[pl.ds(off, SUBLANES), :] * hc + b_ref[pl.ds(off, SUBLANES), :]
        b_ref[pl.ds(off, SUBLANES), :] = hb
        return hb[SUBLANES - 1:SUBLANES, :]

    hcar_ref[...] = lax.fori_loop(0, tt // SUBLANES, blk, hcar_ref[...])
    o_ref[...] = (b_ref[...] * jax.nn.gelu(gate_ref[...].astype(F32))).astype(BF16)


def _lru(proj, b, s, tt, cw, cb, wa, ba, wi, bi, lam):
    c = cw.shape[1]
    nt = s // tt
    row = lambda bb, ss: (bb * nt + ss, 0)
    const2 = lambda bb, ss: (0, 0)
    const3 = lambda bb, ss: (0, 0, 0)
    hd = c // LRU_HEADS
    return pl.pallas_call(
        _lru_kernel,
        grid=(b, nt),
        in_specs=[pl.BlockSpec((tt, c), row),
                  pl.BlockSpec((tt, c), lambda bb, ss: (bb * nt + ss, 1)),
                  pl.BlockSpec((CONV_WIDTH, c), const2),
                  pl.BlockSpec((1, c), const2),
                  pl.BlockSpec((LRU_HEADS, hd, hd), const3),
                  pl.BlockSpec((1, c), const2),
                  pl.BlockSpec((LRU_HEADS, hd, hd), const3),
                  pl.BlockSpec((1, c), const2),
                  pl.BlockSpec((1, c), const2)],
        out_specs=pl.BlockSpec((tt, c), row),
        out_shape=jax.ShapeDtypeStruct((b * s, c), BF16),
        scratch_shapes=[pltpu.VMEM((tt + SUBLANES, c), F32),
                        pltpu.VMEM((tt, c), F32),
                        pltpu.VMEM((tt, c), F32),
                        pltpu.VMEM((1, c), F32)],
        compiler_params=_params(2),
        name="rg_lru",
    )(proj, proj, cw, cb, wa, ba, wi, bi, lam)


def _token_shift(ext_ref, x, mu):
    tt = x.shape[0]
    ext_ref[SUBLANES:SUBLANES + tt, :] = x
    prev = ext_ref[SUBLANES - 1:SUBLANES - 1 + tt, :]
    ext_ref[0:SUBLANES, :] = x[tt - SUBLANES:tt, :]
    return x + (prev - x) * mu


def _store_head_major(o_ref, q, x):
    n = RWKV_HEAD_DIM
    per = LANES // n
    for cc in range(x.shape[1] // LANES):
        xt = x[:, cc * LANES:(cc + 1) * LANES].T
        for hh in range(per):
            h = per * cc + hh
            o_ref[q, h * N_PITCH:h * N_PITCH + n, :] = xt[hh * n:(hh + 1) * n, :]
            o_ref[q, h * N_PITCH + n:(h + 1) * N_PITCH, :] = jnp.zeros((N_PITCH - n, x.shape[0]), F32)


def _rwkv_prep_kernel(r_ref, k_ref, v_ref, l_ref, mur_ref, muk_ref, muv_ref, mul_ref,
                      w0_ref, w2_ref, a0_ref, a2_ref, g2_ref, kk_ref, ka_ref, rk_ref, e_ref,
                      so_ref, go_ref, bonus_ref,
                      er_ref, ek_ref, ev_ref, el_ref):
    @pl.when(pl.program_id(1) == 0)
    def _():
        for ext in (er_ref, ek_ref, ev_ref, el_ref):
            ext[0:SUBLANES, :] = jnp.zeros((SUBLANES, ext.shape[1]), F32)

    r = _token_shift(er_ref, r_ref[...].astype(F32), mur_ref[...])
    k = _token_shift(ek_ref, k_ref[...].astype(F32), muk_ref[...])
    v = _token_shift(ev_ref, v_ref[...].astype(F32), muv_ref[...])
    lo = _token_shift(el_ref, l_ref[...].astype(F32), mul_ref[...])

    lo_a = lo[:, 0:LANES]
    lo_g = lo[:, LANES:3 * LANES]
    wl = w0_ref[...] + jnp.dot(jnp.tanh(lo_a).astype(BF16), w2_ref[...], preferred_element_type=F32)
    w_log = -_softplus(-wl) - 0.5
    decay = jnp.exp(-jnp.exp(w_log))
    a = jax.nn.sigmoid(a0_ref[...] + jnp.dot(lo_a.astype(BF16), a2_ref[...], preferred_element_type=F32))
    g = jnp.dot(jax.nn.sigmoid(lo_g).astype(BF16), g2_ref[...], preferred_element_type=F32)

    e_blk = e_ref[...]
    kk = k * kk_ref[...]
    kk = kk / jnp.maximum(jnp.sqrt(_segsum(kk * kk, e_blk)), 1e-12)
    kmod = k * (1.0 + (a - 1.0) * ka_ref[...])

    for q, val in enumerate((r, decay, kmod, v, -kk, kk * a)):
        _store_head_major(so_ref, q, val)
    go_ref[...] = g
    bonus_ref[...] = _segsum(r * kmod * rk_ref[...], e_blk) * v


def _rwkv_prep(proj, b, s, tt, c, col0, mur, muk, muv, mul, w0, w2p, a0, a2p, g2p, kk, ka, rk, e_blk):
    nt = s // tt
    cb0 = col0 // c
    lb = (col0 + 3 * c) // LORA_PAD
    const2 = lambda bb, ss: (0, 0)
    vec = pl.BlockSpec((1, c), const2)
    row = lambda bb, ss: (bb * nt + ss, 0)
    out = jax.ShapeDtypeStruct((b * s, c), F32)
    hrows = (c // RWKV_HEAD_DIM) * N_PITCH
    return pl.pallas_call(
        _rwkv_prep_kernel,
        grid=(b, nt),
        in_specs=[pl.BlockSpec((tt, c), lambda bb, ss: (bb * nt + ss, cb0)),
                  pl.BlockSpec((tt, c), lambda bb, ss: (bb * nt + ss, cb0 + 1)),
                  pl.BlockSpec((tt, c), lambda bb, ss: (bb * nt + ss, cb0 + 2)),
                  pl.BlockSpec((tt, LORA_PAD), lambda bb, ss: (bb * nt + ss, lb)),
                  vec, vec, vec, pl.BlockSpec((1, LORA_PAD), const2),
                  vec, pl.BlockSpec((LANES, c), const2),
                  vec, pl.BlockSpec((LANES, c), const2),
                  pl.BlockSpec((2 * LANES, c), const2),
                  vec, vec, vec, pl.BlockSpec((LANES, LANES), const2)],
        out_specs=[pl.BlockSpec((N_SCAN_IN, hrows, tt), lambda bb, ss: (0, bb, ss)),
                   pl.BlockSpec((tt, c), row), pl.BlockSpec((tt, c), row)],
        out_shape=[jax.ShapeDtypeStruct((N_SCAN_IN, b * hrows, s), F32), out, out],
        scratch_shapes=[pltpu.VMEM((tt + SUBLANES, c), F32),
                        pltpu.VMEM((tt + SUBLANES, c), F32),
                        pltpu.VMEM((tt + SUBLANES, c), F32),
                        pltpu.VMEM((tt + SUBLANES, LORA_PAD), F32)],
        compiler_params=_params(2),
        name="rwkv_prep",
    )(proj, proj, proj, proj, mur, muk, muv, mul, w0, w2p, a0, a2p, g2p, kk, ka, rk, e_blk)


def _swap_kernel(i_ref, o_ref):
    for n in range(RWKV_HEAD_DIM):
        o_ref[pl.ds(n, LANES, stride=N_PITCH), :] = i_ref[pl.ds(n, LANES, stride=N_PITCH), :].T
    for n in range(RWKV_HEAD_DIM, N_PITCH):
        o_ref[pl.ds(n, LANES, stride=N_PITCH), :] = jnp.zeros((LANES, LANES), F32)


def _to_time_major(p):
    nq, rows, s = p.shape
    return pl.pallas_call(
        _swap_kernel,
        grid=(nq, s // LANES),
        in_specs=[pl.BlockSpec((None, rows, LANES), lambda q, i: (q, 0, i))],
        out_specs=pl.BlockSpec((None, rows, LANES), lambda q, i: (q, i, 0)),
        out_shape=jax.ShapeDtypeStruct((nq, s * N_PITCH, LANES), F32),
        compiler_params=_params(2),
        name="to_time_major",
    )(p)


def _to_head_major(y):
    rows_total, bh = y.shape
    s = rows_total // N_PITCH
    rows = LANES * N_PITCH
    return pl.pallas_call(
        _swap_kernel,
        grid=(s // LANES,),
        in_specs=[pl.BlockSpec((rows, LANES), lambda i: (i, 0))],
        out_specs=pl.BlockSpec((rows, LANES), lambda i: (0, i)),
        out_shape=jax.ShapeDtypeStruct((bh * N_PITCH, s), F32),
        compiler_params=_params(1),
        name="to_head_major",
    )(y)


def _rwkv_scan_kernel(q_ref, y_ref, s_ref, sa_ref):
    n = RWKV_HEAD_DIM
    tt = y_ref.shape[0] // N_PITCH

    @pl.when(pl.program_id(0) == 0)
    def _():
        s_ref[...] = jnp.zeros(s_ref.shape, F32)

    def step(t, carry):
        base = pl.multiple_of(t * N_PITCH, SUBLANES)
        r = q_ref[0, pl.ds(base, n), :]
        w = q_ref[1, pl.ds(base, n), :]
        k = q_ref[2, pl.ds(base, n), :]
        na = q_ref[4, pl.ds(base, n), :]
        b = q_ref[5, pl.ds(base, n), :]
        for i in range(n):
            sa_ref[i:i + 1, :] = jnp.sum(s_ref[i] * na, axis=0, keepdims=True)
        for i in range(n):
            s_new = s_ref[i] * w + sa_ref[i:i + 1, :] * b + q_ref[3, pl.ds(base + i, 1), :] * k
            s_ref[i] = s_new
            y_ref[pl.ds(base + i, 1), :] = jnp.sum(s_new * r, axis=0, keepdims=True)
        y = y_ref[pl.ds(base, n), :]
        mean = jnp.mean(y, axis=0, keepdims=True)
        yc = y - mean
        var = jnp.mean(yc * yc, axis=0, keepdims=True)
        y_ref[pl.ds(base, n), :] = yc * lax.rsqrt(var + GN_EPS)
        y_ref[pl.ds(base + n, N_PITCH - n), :] = jnp.zeros((N_PITCH - n, y_ref.shape[1]), F32)
        return carry

    lax.fori_loop(0, tt, step, 0)


def _rwkv_scan(q, tt):
    nq, rows_total, bh = q.shape
    s = rows_total // N_PITCH
    n = RWKV_HEAD_DIM
    return pl.pallas_call(
        _rwkv_scan_kernel,
        grid=(s // tt,),
        in_specs=[pl.BlockSpec((nq, tt * N_PITCH, bh), lambda i: (0, i, 0))],
        out_specs=pl.BlockSpec((tt * N_PITCH, bh), lambda i: (i, 0)),
        out_shape=jax.ShapeDtypeStruct((rows_total, bh), F32),
        scratch_shapes=[pltpu.VMEM((n, n, bh), F32), pltpu.VMEM((n, bh), F32)],
        compiler_params=_params(1),
        name="rwkv_scan",
    )(q)


def _outproj_kernel(x_ref, ylru_ref, yn_ref, bonus_ref, g_ref, lw_ref, lb_ref, wo1_ref, wo2_ref,
                    n2_ref, x1_ref, h2_ref, h8_ref, sh_ref):
    n = RWKV_HEAD_DIM
    per = LANES // n
    cols = []
    for cc in range(bonus_ref.shape[1] // LANES):
        blk = jnp.concatenate([yn_ref[(per * cc + hh) * N_PITCH:(per * cc + hh) * N_PITCH + n, :]
                               for hh in range(per)], axis=0)
        cols.append(blk.T)
    yn = jnp.concatenate(cols, axis=1)
    y_rwkv = ((yn * lw_ref[...] + lb_ref[...] + bonus_ref[...]) * g_ref[...]).astype(BF16)
    x1 = (x_ref[...]
          + jnp.dot(ylru_ref[...], wo1_ref[...], preferred_element_type=F32)
          + jnp.dot(y_rwkv, wo2_ref[...], preferred_element_type=F32))
    x1_ref[...] = x1
    ms = jnp.mean(x1 * x1, axis=-1, keepdims=True)
    h2 = x1 * lax.rsqrt(ms + NORM_EPS) * n2_ref[...]
    h2_ref[...] = h2.astype(BF16)
    sh = _fp8_scale(jnp.max(jnp.abs(h2), axis=-1, keepdims=True))
    sh_ref[...] = sh
    h8_ref[...] = (h2 / sh).astype(FP8)


def _outproj(x2, ylru, yn, bonus, g, lw, lb, wo1, wo2, n2, tm):
    t, d = x2.shape
    c = ylru.shape[1]
    s = yn.shape[1]
    nst = s // tm
    hrows = (c // RWKV_HEAD_DIM) * N_PITCH
    row_d = pl.BlockSpec((tm, d), lambda i: (i, 0))
    row_c = pl.BlockSpec((tm, c), lambda i: (i, 0))
    const = lambda i: (0, 0)
    return pl.pallas_call(
        _outproj_kernel,
        grid=(t // tm,),
        in_specs=[row_d, row_c, pl.BlockSpec((hrows, tm), lambda i: (i // nst, i % nst)), row_c, row_c,
                  pl.BlockSpec((1, c), const), pl.BlockSpec((1, c), const),
                  pl.BlockSpec((c, d), const), pl.BlockSpec((c, d), const),
                  pl.BlockSpec((1, d), const)],
        out_specs=[row_d, row_d, row_d, pl.BlockSpec((tm, 1), lambda i: (i, 0))],
        out_shape=[jax.ShapeDtypeStruct((t, d), F32), jax.ShapeDtypeStruct((t, d), BF16),
                   jax.ShapeDtypeStruct((t, d), FP8), jax.ShapeDtypeStruct((t, 1), F32)],
        compiler_params=_params(1),
        name="outproj",
    )(x2, ylru, yn, bonus, g, lw, lb, wo1, wo2, n2)


def _matmul_kernel(a_ref, w_ref, o_ref):
    o_ref[...] = jnp.dot(a_ref[...], w_ref[...], preferred_element_type=F32)


def _qproj(h2, wq, tm):
    t, d = h2.shape
    n = wq.shape[1]
    return pl.pallas_call(
        _matmul_kernel,
        grid=(t // tm,),
        in_specs=[pl.BlockSpec((tm, d), lambda i: (i, 0)),
                  pl.BlockSpec((d, n), lambda i: (0, 0))],
        out_specs=pl.BlockSpec((tm, n), lambda i: (i, 0)),
        out_shape=jax.ShapeDtypeStruct((t, n), F32),
        compiler_params=_params(1),
        name="qproj",
    )(h2, wq)


def _dot_nt3(a, b):
    dn = (((1,), (1,)), ((), ()))
    ah, al = _split_bf16(a)
    bh, bl = _split_bf16(b)
    return (lax.dot_general(ah, bh, dn, preferred_element_type=F32)
            + lax.dot_general(ah, bl, dn, preferred_element_type=F32)
            + lax.dot_general(al, bh, dn, preferred_element_type=F32))


def _tree(op, xs):
    xs = list(xs)
    while len(xs) > 1:
        nxt = [op(xs[i], xs[i + 1]) for i in range(0, len(xs) - 1, 2)]
        if len(xs) % 2:
            nxt.append(xs[-1])
        xs = nxt
    return xs[0]


def _route_kernel(q_ref, keys_ref, g_ref, ii_ref, jj_ref):
    tm = q_ref.shape[0]
    nk = PEER_NKEYS
    neg = -jnp.inf
    iota_n = lax.broadcasted_iota(jnp.int32, (nk, tm), 0).astype(F32)

    vals = [[None, None] for _ in range(PEER_HEADS)]
    idxs = [[None, None] for _ in range(PEER_HEADS)]
    for h in range(PEER_HEADS):
        for p in range(2):
            hp = 2 * h + p
            cur = _dot_nt3(keys_ref[hp], q_ref[:, hp * nk:(hp + 1) * nk])
            vs, ix = [], []
            for _ in range(PEER_TOPK):
                m = jnp.max(cur, axis=0, keepdims=True)
                idx = jnp.min(jnp.where(cur == m, iota_n, float(nk)), axis=0, keepdims=True)
                vs.append(m)
                ix.append(idx)
                cur = jnp.where(iota_n == idx, neg, cur)
            vals[h][p] = vs
            idxs[h][p] = ix

    def by_rank(table, p, rnk):
        return jnp.concatenate([table[h][p][rnk] for h in range(PEER_HEADS)], axis=0)

    va = [by_rank(vals, 0, r) for r in range(PEER_TOPK)]
    vb = [by_rank(vals, 1, r) for r in range(PEER_TOPK)]
    ia = [by_rank(idxs, 0, r) for r in range(PEER_TOPK)]
    ib = [by_rank(idxs, 1, r) for r in range(PEER_TOPK)]

    cand = []
    for p in range(PEER_TOPK):
        for qq in range(PEER_TOPK // (p + 1)):
            cand.append([va[p] + vb[qq], ia[p], ib[qq]])
    ncand = len(cand)

    best, sel_i, sel_j = [], [], []
    for _ in range(PEER_TOPK):
        m = _tree(jnp.maximum, [c[0] for c in cand])
        code = _tree(jnp.minimum, [jnp.where(c[0] == m, float(ci), float(ncand))
                                   for ci, c in enumerate(cand)])
        ci_sel = jnp.zeros_like(m)
        cj_sel = jnp.zeros_like(m)
        for ci, c in enumerate(cand):
            win = code == float(ci)
            ci_sel = jnp.where(win, c[1], ci_sel)
            cj_sel = jnp.where(win, c[2], cj_sel)
            c[0] = jnp.where(win, neg, c[0])
        best.append(m)
        sel_i.append(ci_sel)
        sel_j.append(cj_sel)

    ex = [jnp.exp(bv - best[0]) for bv in best]
    z = _tree(jnp.add, ex)
    gates = jnp.concatenate([e / z for e in ex], axis=0)
    g_ref[...] = gates.T
    ii_ref[...] = jnp.concatenate(sel_i, axis=0).T
    jj_ref[...] = jnp.concatenate(sel_j, axis=0).T


def _route(q, keys, tm):
    t, dq = q.shape
    nslot = PEER_TOPK * PEER_HEADS
    out = jax.ShapeDtypeStruct((t, nslot), F32)
    oblk = pl.BlockSpec((tm, nslot), lambda i: (i, 0))
    return pl.pallas_call(
        _route_kernel,
        grid=(t // tm,),
        in_specs=[pl.BlockSpec((tm, dq), lambda i: (i, 0)),
                  pl.BlockSpec(keys.shape, lambda i: (0, 0, 0))],
        out_specs=[oblk, oblk, oblk],
        out_shape=[out, out, out],
        compiler_params=_params(1),
        name="peer_route",
    )(q, keys)


def _wbuild_kernel(g_ref, ii_ref, jj_ref, sv_ref, w_ref, buf_ref):
    tw = g_ref.shape[0]
    nk = PEER_NKEYS
    iota_s = lax.broadcasted_iota(jnp.int32, (nk, g_ref.shape[1]), 0).astype(F32)
    dn = (((1,), (1,)), ((), ()))

    def token_block(t):
        g = g_ref[pl.ds(t, 1), :]
        a = jnp.where(iota_s == ii_ref[pl.ds(t, 1), :], g, 0.0).astype(BF16)
        bt = jnp.where(iota_s == jj_ref[pl.ds(t, 1), :], 1.0, 0.0).astype(BF16)
        return lax.dot_general(a, bt, dn, preferred_element_type=F32) * sv_ref[...]

    def tok(t, carry):
        off = pl.multiple_of(t * W_PITCH, SUBLANES)
        buf_ref[pl.ds(off, nk), :] = token_block(t)
        return carry

    lax.fori_loop(0, tw, tok, 0, unroll=TOKEN_UNROLL)
    per = w_ref.shape[2] // nk
    for i in range(nk):
        w_ref[i // per, :, (i % per) * nk:(i % per + 1) * nk] = (
            buf_ref[pl.ds(i, tw, stride=W_PITCH), :].astype(BF16))


def _wbuild(gates, ii, jj, sv, tw, te):
    t, nslot = gates.shape
    ne = PEER_NKEYS * PEER_NKEYS
    blk = pl.BlockSpec((tw, nslot), lambda i: (i, 0))
    return pl.pallas_call(
        _wbuild_kernel,
        grid=(t // tw,),
        in_specs=[blk, blk, blk, pl.BlockSpec((PEER_NKEYS, PEER_NKEYS), lambda i: (0, 0))],
        out_specs=pl.BlockSpec((ne // te, tw, te), lambda i: (0, i, 0)),
        out_shape=jax.ShapeDtypeStruct((ne // te, t, te), BF16),
        scratch_shapes=[pltpu.VMEM((tw * W_PITCH, PEER_NKEYS), F32)],
        compiler_params=_params(1),
        name="peer_weights",
    )(gates, ii, jj, sv.reshape(PEER_NKEYS, PEER_NKEYS))


def _peer_step(h_ref, sh_ref, u_ref, su_ref, v_ref, w_ref, o_ref,
               act_new, act_old, p_new, s_new, p_old, s_old):
    tm = h_ref.shape[0]
    rc = tm // PEER_ROW_CHUNKS
    for c in range(PEER_ROW_CHUNKS):
        rows = slice(c * rc, (c + 1) * rc)
        a = act_old[rows, :] * sh_ref[rows, :] * su_ref[...]
        p = _gelu_tanh_bf16(a) * w_ref[rows, :]
        sp = _fp8_scale(jnp.max(jnp.abs(p), axis=-1, keepdims=True).astype(F32))
        p_new[rows, :] = (p.astype(F32) / sp).astype(FP8)
        s_new[rows, :] = sp
        act_new[rows, :] = lax.dot_general(h_ref[rows, :], u_ref[...], (((1,), (1,)), ((), ())),
                                           preferred_element_type=F32)
        o_ref[rows, :] += jnp.dot(p_old[rows, :], v_ref[...], preferred_element_type=F32) * s_old[rows, :]


def _peer_kernel(h_ref, sh_ref, u_ref, su_ref, v_ref, w_ref, x1_ref, gf_ref, o_ref,
                 act0_ref, act1_ref, p0_ref, s0_ref, p1_ref, s1_ref, *, ne, final_norm):
    g = pl.program_id(0)
    e_out = (g - 2) % ne

    @pl.when(g == 0)
    def _():
        act1_ref[...] = jnp.zeros(act1_ref.shape, F32)
        p0_ref[...] = jnp.zeros(p0_ref.shape, FP8)
        s0_ref[...] = jnp.ones(s0_ref.shape, F32)

    @pl.when((g < 2) | (e_out == 0))
    def _():
        o_ref[...] = jnp.zeros(o_ref.shape, F32)

    args = (h_ref, sh_ref, u_ref, su_ref, v_ref, w_ref, o_ref)

    @pl.when(g % 2 == 0)
    def _():
        _peer_step(*args, act0_ref, act1_ref, p1_ref, s1_ref, p0_ref, s0_ref)

    @pl.when(g % 2 == 1)
    def _():
        _peer_step(*args, act1_ref, act0_ref, p0_ref, s0_ref, p1_ref, s1_ref)

    @pl.when((g >= 2) & (e_out == ne - 1))
    def _():
        x = x1_ref[...] + o_ref[...]
        if final_norm:
            ms = jnp.mean(x * x, axis=-1, keepdims=True)
            x = x * lax.rsqrt(ms + NORM_EPS) * gf_ref[...]
        o_ref[...] = x


def _peer(h8, sh, u8, su, v8, w, x1, gf, final_norm, tm, te):
    t, d = h8.shape
    ne = u8.shape[0] // te
    nt = t // tm
    tok = lambda g: jnp.clip(g // ne, 0, nt - 1)
    return pl.pallas_call(
        functools.partial(_peer_kernel, ne=ne, final_norm=final_norm),
        grid=(nt * ne + 2,),
        in_specs=[pl.BlockSpec((tm, d), lambda g: (tok(g), 0)),
                  pl.BlockSpec((tm, 1), lambda g: (tok(g - 1), 0)),
                  pl.BlockSpec((te, d), lambda g: (g % ne, 0)),
                  pl.BlockSpec((1, te), lambda g: (0, (g - 1) % ne)),
                  pl.BlockSpec((te, d), lambda g: ((g - 2) % ne, 0)),
                  pl.BlockSpec((None, tm, te), lambda g: ((g - 1) % ne, tok(g - 1), 0)),
                  pl.BlockSpec((tm, d), lambda g: (tok(g - 2), 0), pipeline_mode=pl.Buffered(1)),
                  pl.BlockSpec((1, d), lambda g: (0, 0))],
        out_specs=pl.BlockSpec((tm, d), lambda g: (tok(g - 2), 0), pipeline_mode=pl.Buffered(1)),
        out_shape=jax.ShapeDtypeStruct((t, d), F32),
        scratch_shapes=[pltpu.VMEM((tm, te), F32), pltpu.VMEM((tm, te), F32),
                        pltpu.VMEM((tm, te), FP8), pltpu.VMEM((tm, 1), F32),
                        pltpu.VMEM((tm, te), FP8), pltpu.VMEM((tm, 1), F32)],
        compiler_params=pltpu.CompilerParams(dimension_semantics=("arbitrary",),
                                             vmem_limit_bytes=PEER_VMEM_LIMIT),
        name="peer_experts",
    )(h8, sh, u8, su, v8, w, x1, gf)


def _quantize_kernel(t_ref, q_ref, s_ref):
    x = t_ref[...]
    scale = _fp8_scale(jnp.max(jnp.abs(x), axis=1, keepdims=True))
    s_ref[...] = scale
    q_ref[...] = (x / scale).astype(FP8)


def _quantize_rows(tab, tr):
    n, d = tab.shape
    q, scale = pl.pallas_call(
        _quantize_kernel,
        grid=(n // tr,),
        in_specs=[pl.BlockSpec((tr, d), lambda i: (i, 0))],
        out_specs=[pl.BlockSpec((tr, d), lambda i: (i, 0)), pl.BlockSpec((tr, 1), lambda i: (i, 0))],
        out_shape=[jax.ShapeDtypeStruct((n, d), FP8), jax.ShapeDtypeStruct((n, 1), F32)],
        compiler_params=_params(1),
        name="quantize_rows",
    )(tab)
    return q, scale.reshape(1, n)


def _tiles(t, s):
    return dict(
        inproj_tm=min(1024, t), inproj_tn=512,
        seq_tt=min(256, s),
        scan_tt=min(32, s),
        outproj_tm=min(256, s),
        qproj_tm=min(512, t),
        route_tm=min(256, t),
        wbuild_tw=min(128, t),
        peer_tm=min(1024, t), peer_te=1024,
        quant_tr=512,
    )


def _layer(x2, b, s, final_g, norm1_g, w_in, conv_w, conv_b, lru_wa, lru_ba, lru_wi, lru_bi, lru_lambda,
           rwkv_mu, rwkv_w0, rwkv_w2, rwkv_a0, rwkv_a2, rwkv_g2, rwkv_k_k, rwkv_k_a, rwkv_r_k,
           rwkv_lnx_w, rwkv_lnx_b, w_out, norm2_g, peer_wq, peer_keys, peer_u, peer_v):
    t, d = x2.shape
    c_lru = conv_w.shape[1]
    c_rwkv = rwkv_w0.shape[0]
    n_heads = c_rwkv // RWKV_HEAD_DIM
    tl = _tiles(t, s)
    row = lambda v: v.reshape(1, -1)

    n_lora = DECAY_LORA + AAA_LORA + GATE_LORA
    w_in_p = jnp.pad(w_in, ((0, 0), (0, LORA_PAD - n_lora))).astype(BF16)
    proj = _inproj(x2, row(norm1_g), w_in_p, tl["inproj_tm"], tl["inproj_tn"])

    y_lru = _lru(proj, b, s, tl["seq_tt"], conv_w, row(conv_b), lru_wa.astype(BF16), row(lru_ba),
                 lru_wi.astype(BF16), row(lru_bi), row(lru_lambda))

    mu_main, mu_lora = rwkv_mu[:3 * c_rwkv], rwkv_mu[3 * c_rwkv:]
    mu_l = jnp.pad(mu_lora, (0, LORA_PAD - n_lora))
    w2p = jnp.pad(rwkv_w2, ((0, LANES - DECAY_LORA), (0, 0))).astype(BF16)
    a2p = jnp.pad(rwkv_a2, ((DECAY_LORA, LANES - DECAY_LORA - AAA_LORA), (0, 0))).astype(BF16)
    g2p = jnp.pad(rwkv_g2, ((0, 2 * LANES - GATE_LORA), (0, 0))).astype(BF16)
    seg = np.arange(LANES) // RWKV_HEAD_DIM
    e_blk = jnp.asarray(seg[:, None] == seg[None, :], dtype=BF16)
    assert b * n_heads == LANES, "the scan maps (batch, head) onto the 128 lanes"
    scan_in, g_s, bonus = _rwkv_prep(
        proj, b, s, tl["seq_tt"], c_rwkv, 2 * c_lru,
        row(mu_main[:c_rwkv]), row(mu_main[c_rwkv:2 * c_rwkv]), row(mu_main[2 * c_rwkv:]), row(mu_l),
        row(rwkv_w0), w2p, row(rwkv_a0), a2p, g2p, row(rwkv_k_k), row(rwkv_k_a),
        row(rwkv_r_k), e_blk)

    yn = _to_head_major(_rwkv_scan(_to_time_major(scan_in), tl["scan_tt"]))

    w_out_bf = w_out.astype(BF16)
    x1, h2, h8, sh = _outproj(x2, y_lru, yn, bonus, g_s, row(rwkv_lnx_w), row(rwkv_lnx_b),
                              w_out_bf[:c_lru], w_out_bf[c_lru:], row(norm2_g), tl["outproj_tm"])

    q = _qproj(h2, peer_wq.astype(BF16), tl["qproj_tm"])
    keys = peer_keys.reshape(2 * PEER_HEADS, PEER_NKEYS, -1)
    gates, ii, jj = _route(q, keys, tl["route_tm"])
    u8, su = _quantize_rows(peer_u, tl["quant_tr"])
    v8, sv = _quantize_rows(peer_v, tl["quant_tr"])
    w = _wbuild(gates, ii, jj, sv, tl["wbuild_tw"], tl["peer_te"])
    gf = jnp.ones((1, d), F32) if final_g is None else final_g.reshape(1, d)
    return _peer(h8, sh, u8, su, v8, w, x1, gf, final_g is not None, tl["peer_tm"], tl["peer_te"])


def kernel(x, norm1_g, w_in, conv_w, conv_b, lru_wa, lru_ba, lru_wi, lru_bi, lru_lambda, rwkv_mu, rwkv_w0, rwkv_w2, rwkv_a0, rwkv_a2, rwkv_g2, rwkv_k_k, rwkv_k_a, rwkv_r_k, rwkv_lnx_w, rwkv_lnx_b, w_out, norm2_g, peer_wq, peer_keys, peer_u, peer_v, normf_g):
    b, s, d = x.shape
    depth = norm1_g.shape[0]
    x2 = x.reshape(b * s, d)
    for l in range(depth):
        final_g = normf_g if l == depth - 1 else None
        x2 = _layer(x2, b, s, final_g, norm1_g[l], w_in[l], conv_w[l], conv_b[l], lru_wa[l], lru_ba[l],
                    lru_wi[l], lru_bi[l], lru_lambda[l], rwkv_mu[l], rwkv_w0[l], rwkv_w2[l],
                    rwkv_a0[l], rwkv_a2[l], rwkv_g2[l], rwkv_k_k[l], rwkv_k_a[l],
                    rwkv_r_k[l].reshape(-1), rwkv_lnx_w[l], rwkv_lnx_b[l], w_out[l], norm2_g[l],
                    peer_wq[l], peer_keys[l], peer_u[l], peer_v[l])
    return x2.reshape(b, s, d)
```
